```python
import jax, jax.numpy as jnp
from jax import lax
import numpy as np

D_MODEL = 2048
BATCH = 4
SEQ = 4096
DEPTH = 4

HEAD_DIM = 128
ATTN_WIDTH = D_MODEL // 2
N_Q_HEADS = ATTN_WIDTH // HEAD_DIM
N_KV_HEADS = N_Q_HEADS // 4
KV_WIDTH = N_KV_HEADS * HEAD_DIM
CONV_WIDTH = D_MODEL // 4
CONV_KERNEL = 31
SGU_WIDTH = D_MODEL // 4
SGU_HEAD_DIM = 128
SGU_HEADS = SGU_WIDTH // SGU_HEAD_DIM
CHUNK = 128
MIX_WIDTH = ATTN_WIDTH + CONV_WIDTH + SGU_WIDTH
IN_WIDTH = ATTN_WIDTH + 2 * KV_WIDTH + 2 * CONV_WIDTH + 2 * SGU_WIDTH
WINDOW = 128
BLOCK = 128
ROPE_THETA = 500000.0
ROT_DIM = HEAD_DIM // 4
D_FF = ((8 * D_MODEL // 3 + 255) // 256) * 256
EPS = 1e-6

kernel_name = "hybrid_parallel_groups_encoder"


def rms_norm(x, g):
    xf = x.astype(jnp.float32)
    y = xf * lax.rsqrt(jnp.mean(xf * xf, axis=-1, keepdims=True) + EPS)
    return (y * g.astype(jnp.float32)).astype(x.dtype)


def layer_norm(x, g, b):
    xf = x.astype(jnp.float32)
    mu = jnp.mean(xf, axis=-1, keepdims=True)
    var = jnp.mean(jnp.square(xf - mu), axis=-1, keepdims=True)
    y = (xf - mu) * lax.rsqrt(var + EPS)
    return (y * g.astype(jnp.float32) + b.astype(jnp.float32)).astype(x.dtype)


def rope_tables(seq):
    pos = jnp.arange(seq, dtype=jnp.float32)
    inv = ROPE_THETA ** (-jnp.arange(0, ROT_DIM, 2, dtype=jnp.float32) / ROT_DIM)
    ang = pos[:, None] * inv[None, :]
    return jnp.cos(ang), jnp.sin(ang)


def partial_rope(t, cos, sin):
    half = ROT_DIM // 2
    t1 = t[..., :half].astype(jnp.float32)
    t2 = t[..., half:ROT_DIM].astype(jnp.float32)
    c = cos[None, :, None, :]
    s = sin[None, :, None, :]
    rot = jnp.concatenate([t1 * c - t2 * s, t2 * c + t1 * s], axis=-1).astype(t.dtype)
    return jnp.concatenate([rot, t[..., ROT_DIM:]], axis=-1)


def windowed_gqa_sink(q, k, v, sink):
    B, S, H, Dh = q.shape
    G = k.shape[2]
    R = H // G
    nb = S // BLOCK
    pad = ((0, 0), (BLOCK, BLOCK), (0, 0), (0, 0))
    kp = jnp.pad(k, pad).reshape(B, nb + 2, BLOCK, G, Dh)
    vp = jnp.pad(v, pad).reshape(B, nb + 2, BLOCK, G, Dh)
    kw = jnp.concatenate([kp[:, :-2], kp[:, 1:-1], kp[:, 2:]], axis=2)
    vw = jnp.concatenate([vp[:, :-2], vp[:, 1:-1], vp[:, 2:]], axis=2)
    qb = q.reshape(B, nb, BLOCK, G, R, Dh)
    scale = 1.0 / float(np.sqrt(Dh))
    s = jnp.einsum('bnqgrd,bnkgd->bngrqk', qb, kw).astype(jnp.float32) * scale
    qpos = jnp.arange(S).reshape(nb, BLOCK)
    kpos = jnp.arange(nb)[:, None] * BLOCK - BLOCK + jnp.arange(3 * BLOCK)[None, :]
    valid = ((kpos[:, None, :] >= 0) & (kpos[:, None, :] < S)
             & (jnp.abs(qpos[:, :, None] - kpos[:, None, :]) <= WINDOW))
    s = jnp.where(valid[None, :, None, None], s, jnp.finfo(jnp.float32).min)
    sk = jnp.broadcast_to(sink.astype(jnp.float32).reshape(1, 1, G, R, 1, 1), s.shape[:-1] + (1,))
    p = jax.nn.softmax(jnp.concatenate([s, sk], axis=-1), axis=-1)[..., :-1]
    o = jnp.einsum('bngrqk,bnkgd->bnqgrd', p.astype(v.dtype), vw)
    return o.reshape(B, S, H * Dh)


def conformer_conv(a, gate, dw_w, dw_b, ln_g, ln_b):
    c = a * jax.nn.sigmoid(gate)
    C = c.shape[-1]
    c = lax.conv_general_dilated(
        c, dw_w[:, None, :].astype(c.dtype), window_strides=(1,),
        padding=[((CONV_KERNEL - 1) // 2, (CONV_KERNEL - 1) // 2)],
        dimension_numbers=('NWC', 'WIO', 'NWC'), feature_group_count=C) + dw_b
    c = layer_norm(c, ln_g, ln_b)
    return jax.nn.silu(c)


def spatial_gating(uv, ln_g, ln_b, w_s, b_s):
    B, S, _ = uv.shape
    uv = jax.nn.gelu(uv, approximate=False)
    u, v = jnp.split(uv, 2, axis=-1)
    v = layer_norm(v, ln_g, ln_b)
    vc = v.reshape(B, S // CHUNK, CHUNK, SGU_HEADS, SGU_HEAD_DIM)
    sp = jnp.einsum('hpq,bcqhe->bcphe', w_s, vc) + b_s.T[None, None, :, :, None]
    return u * sp.reshape(B, S, SGU_WIDTH)


def hybrid_mixer(h, w_in, sink, dw_w, dw_b, cln_g, cln_b, sln_g, sln_b, sgu_w, sgu_b, w_out, cos, sin):
    B, S, _ = h.shape
    z = h @ w_in
    offs = np.cumsum([ATTN_WIDTH, KV_WIDTH, KV_WIDTH, CONV_WIDTH, CONV_WIDTH]).tolist()
    q, k, v, ca, cg, uv = jnp.split(z, offs, axis=-1)
    q = partial_rope(q.reshape(B, S, N_Q_HEADS, HEAD_DIM), cos, sin)
    k = partial_rope(k.reshape(B, S, N_KV_HEADS, HEAD_DIM), cos, sin)
    v = v.reshape(B, S, N_KV_HEADS, HEAD_DIM)
    attn = windowed_gqa_sink(q, k, v, sink)
    conv = conformer_conv(ca, cg, dw_w, dw_b, cln_g, cln_b)
    sgu = spatial_gating(uv, sln_g, sln_b, sgu_w, sgu_b)
    return jnp.concatenate([attn.astype(h.dtype), conv.astype(h.dtype), sgu.astype(h.dtype)], axis=-1) @ w_out


def swiglu(h, w_gate, w_up, w_down):
    return (jax.nn.silu(h @ w_gate) * (h @ w_up)) @ w_down


def setup_inputs(seed: int = 0) -> dict:
    key = jax.random.key(seed)
    ks = jax.random.split(key, 20)
    f32 = jnp.float32
    nrm = lambda k, shape, sc: jax.random.normal(k, shape, f32) * sc
    return {
        "x": nrm(ks[0], (BATCH, SEQ, D_MODEL), 1.0),
        "mix_norm_g": 1.0 + nrm(ks[1], (DEPTH, D_MODEL), 0.02),
        "w_in": nrm(ks[2], (DEPTH, D_MODEL, IN_WIDTH), D_MODEL ** -0.5),
        "sink": nrm(ks[3], (DEPTH, N_Q_HEADS), 0.5),
        "conv_dw_w": nrm(ks[4], (DEPTH, CONV_KERNEL, CONV_WIDTH), CONV_KERNEL ** -0.5),
        "conv_dw_b": nrm(ks[5], (DEPTH, CONV_WIDTH), 0.02),
        "conv_ln_g": 1.0 + nrm(ks[6], (DEPTH, CONV_WIDTH), 0.02),
        "conv_ln_b": nrm(ks[7], (DEPTH, CONV_WIDTH), 0.02),
        "sgu_ln_g": 1.0 + nrm(ks[8], (DEPTH, SGU_WIDTH), 0.02),
        "sgu_ln_b": nrm(ks[9], (DEPTH, SGU_WIDTH), 0.02),
        "sgu_w": nrm(ks[10], (DEPTH, SGU_HEADS, CHUNK, CHUNK), CHUNK ** -0.5),
        "sgu_b": 1.0 + nrm(ks[11], (DEPTH, SGU_HEADS, CHUNK), 0.02),
        "w_out": nrm(ks[12], (DEPTH, MIX_WIDTH, D_MODEL), MIX_WIDTH ** -0.5),
        "ffn_norm_g": 1.0 + nrm(ks[13], (DEPTH, D_MODEL), 0.02),
        "w_gate": nrm(ks[14], (DEPTH, D_MODEL, D_FF), D_MODEL ** -0.5),
        "w_up": nrm(ks[15], (DEPTH, D_MODEL, D_FF), D_MODEL ** -0.5),
        "w_down": nrm(ks[16], (DEPTH, D_FF, D_MODEL), D_FF ** -0.5),
        "final_norm_g": 1.0 + nrm(ks[17], (D_MODEL,), 0.02),
    }


def reference(x, mix_norm_g, w_in, sink, conv_dw_w, conv_dw_b, conv_ln_g, conv_ln_b,
              sgu_ln_g, sgu_ln_b, sgu_w, sgu_b, w_out, ffn_norm_g, w_gate, w_up, w_down,
              final_norm_g):
    cos, sin = rope_tables(x.shape[1])
    for l in range(DEPTH):
        h = rms_norm(x, mix_norm_g[l])
        x = x + hybrid_mixer(h, w_in[l], sink[l], conv_dw_w[l], conv_dw_b[l], conv_ln_g[l],
                             conv_ln_b[l], sgu_ln_g[l], sgu_ln_b[l], sgu_w[l], sgu_b[l],
                             w_out[l], cos, sin)
        h = rms_norm(x, ffn_norm_g[l])
        x = x + swiglu(h, w_gate[l], w_up[l], w_down[l])
    return rms_norm(x, final_norm_g)
```

```python
import functools

import jax
import jax.numpy as jnp
import numpy as np
from jax import lax
from jax.experimental import pallas as pl
from jax.experimental.pallas import tpu as pltpu

F32 = jnp.float32
BF16 = jnp.bfloat16

D_MODEL = 2048
DEPTH = 4
HEAD_DIM = 128
ATTN_WIDTH = D_MODEL // 2
N_Q_HEADS = ATTN_WIDTH // HEAD_DIM
N_KV_HEADS = N_Q_HEADS // 4
Q_PER_KV = N_Q_HEADS // N_KV_HEADS
KV_WIDTH = N_KV_HEADS * HEAD_DIM
CONV_WIDTH = D_MODEL // 4
CONV_KERNEL = 31
CONV_PAD = (CONV_KERNEL - 1) // 2
SGU_WIDTH = D_MODEL // 4
SGU_HEADS = SGU_WIDTH // HEAD_DIM
CHUNK = 128
IN_WIDTH = ATTN_WIDTH + 2 * KV_WIDTH + 2 * CONV_WIDTH + 2 * SGU_WIDTH
WINDOW = 128
BLOCK = 128
ROPE_THETA = 500000.0
ROT_DIM = HEAD_DIM // 4
ROT_HALF = ROT_DIM // 2
D_FF = ((8 * D_MODEL // 3 + 255) // 256) * 256
EPS = 1e-6

K_START = ATTN_WIDTH
V_START = K_START + KV_WIDTH
CA_START = V_START + KV_WIDTH
CG_START = CA_START + CONV_WIDTH
UV_START = CG_START + CONV_WIDTH

V7X_VMEM_LIMIT_BYTES = 60 * 1024 * 1024
BF16_SUBLANE_TILE = 16
F32_SUBLANE_TILE = 8
TM_PROJ = 512
TB_MIX = 512
TM_FFN = 512
TF_FFN = 512
CONV_ROWS = 32
MASK_VALUE = -1e30


def _rms_norm_rows(x, g):
    ms = jnp.mean(x * x, axis=-1, keepdims=True)
    return x * lax.rsqrt(ms + EPS) * g


def _layer_norm_rows(x, g, b):
    mu = jnp.mean(x, axis=-1, keepdims=True)
    xc = x - mu
    var = jnp.mean(xc * xc, axis=-1, keepdims=True)
    return xc * lax.rsqrt(var + EPS) * g + b


def _rope_tables(seq):
    pos = jnp.arange(seq, dtype=F32)
    inv = ROPE_THETA ** (-jnp.arange(0, ROT_DIM, 2, dtype=F32) / ROT_DIM)
    ang = pos[:, None] * inv[None, :]
    cos, sin = jnp.cos(ang), jnp.sin(ang)
    rest = HEAD_DIM - ROT_DIM
    cos_t = jnp.concatenate([cos, cos, jnp.ones((seq, rest), F32)], axis=-1)
    sin_a = jnp.concatenate([-sin, jnp.zeros((seq, HEAD_DIM - ROT_HALF), F32)], axis=-1)
    sin_b = jnp.concatenate([jnp.zeros((seq, ROT_HALF), F32), sin, jnp.zeros((seq, rest), F32)], axis=-1)
    return cos_t, sin_a, sin_b


def _in_proj_kernel(x_ref, g_ref, cos_ref, sa_ref, sb_ref, w_ref, q_ref, kv_ref, c_ref, gu_ref, h_scr):
    h_scr[...] = _rms_norm_rows(x_ref[...], g_ref[...]).astype(BF16)
    cos, sin_a, sin_b = cos_ref[...], sa_ref[...], sb_ref[...]

    def rope(t):
        return (t * cos + pltpu.roll(t, HEAD_DIM - ROT_HALF, 1) * sin_a
                + pltpu.roll(t, ROT_HALF, 1) * sin_b)

    def proj(lo, hi):
        return jnp.dot(h_scr[...], w_ref[:, lo:hi], preferred_element_type=F32)

    zq = proj(0, K_START)
    for hd in range(N_Q_HEADS):
        sl = slice(hd * HEAD_DIM, (hd + 1) * HEAD_DIM)
        q_ref[:, sl] = rope(zq[:, sl]).astype(BF16)
    zk = proj(K_START, V_START)
    for hd in range(N_KV_HEADS):
        sl = slice(hd * HEAD_DIM, (hd + 1) * HEAD_DIM)
        kv_ref[:, sl] = rope(zk[:, sl]).astype(BF16)
    kv_ref[:, KV_WIDTH:] = proj(V_START, CA_START).astype(BF16)
    za = proj(CA_START, CG_START)
    zg = proj(CG_START, UV_START)
    c_ref[...] = (za * jax.nn.sigmoid(zg)).astype(BF16)
    zu = proj(UV_START, IN_WIDTH)
    gelu = 0.5 * zu * (1.0 + lax.erf(zu * np.float32(np.sqrt(0.5))))
    gu_ref[...] = gelu.astype(BF16)


def _in_proj(layer, x, norm_g, ropes, w_in, seq):
    tokens = x.shape[0]
    tm = TM_PROJ
    seq_blocks = seq // tm
    row = lambda i: (i, 0)
    rope_spec = pl.BlockSpec((tm, HEAD_DIM), lambda i: (i % seq_blocks, 0))
    out_w = (ATTN_WIDTH, 2 * KV_WIDTH, CONV_WIDTH, 2 * SGU_WIDTH)
    return pl.pallas_call(
        _in_proj_kernel,
        grid=(tokens // tm,),
        in_specs=[
            pl.BlockSpec((tm, D_MODEL), row),
            pl.BlockSpec((None, 1, D_MODEL), lambda i: (layer, 0, 0)),
            rope_spec, rope_spec, rope_spec,
            pl.BlockSpec((None, D_MODEL, IN_WIDTH), lambda i: (layer, 0, 0),
                         pipeline_mode=pl.Buffered(1)),
        ],
        out_specs=[pl.BlockSpec((tm, w), row) for w in out_w],
        out_shape=[jax.ShapeDtypeStruct((tokens, w), BF16) for w in out_w],
        scratch_shapes=[pltpu.VMEM((tm, D_MODEL), BF16)],
        compiler_params=pltpu.CompilerParams(
            dimension_semantics=("arbitrary",), vmem_limit_bytes=V7X_VMEM_LIMIT_BYTES),
        name=f"in_proj_l{layer}",
    )(x, norm_g, *ropes, w_in)


def _mixer_kernel(layer, sink_ref, q_ref, kvc_ref, kvp_ref, kvn_ref, cc_ref, cp_ref, cn_ref, gu_ref,
                  dww_ref, dwb_ref, clg_ref, clb_ref, slg_ref, slb_ref, sw_ref, sbt_ref,
                  o_ref, kv_scr, c_scr):
    tb = q_ref.shape[0]
    i = pl.program_id(1)
    is_first = (i == 0).astype(jnp.int32)
    is_last = (i == pl.num_programs(1) - 1).astype(jnp.int32)

    kv_scr[0:BLOCK, :] = kvp_ref[...]
    kv_scr[BLOCK:BLOCK + tb, :] = kvc_ref[...]
    kv_scr[BLOCK + tb:, :] = kvn_ref[...]
    r = lax.broadcasted_iota(jnp.int32, (BLOCK, 3 * BLOCK), 0)
    c = lax.broadcasted_iota(jnp.int32, (BLOCK, 3 * BLOCK), 1)
    in_window = (c - (WINDOW + BLOCK) <= r) & (r <= c - (BLOCK - WINDOW))
    bias_mid = jnp.where(in_window, 0.0, MASK_VALUE).astype(F32)
    bias_first = jnp.where(c < BLOCK * is_first, MASK_VALUE, bias_mid)
    bias_last = jnp.where(c >= 3 * BLOCK - BLOCK * is_last, MASK_VALUE, bias_mid)
    scale = np.float32(1.0 / np.sqrt(HEAD_DIM))
    n_sub = tb // BLOCK
    for j in range(n_sub):
        bias = bias_first if j == 0 else (bias_last if j == n_sub - 1 else bias_mid)
        if n_sub == 1:
            bias = jnp.minimum(bias_first, bias_last)
        rows = slice(j * BLOCK, (j + 1) * BLOCK)
        win = slice(j * BLOCK, (j + 3) * BLOCK)
        for g in range(N_KV_HEADS):
            heads = range(g * Q_PER_KV, (g + 1) * Q_PER_KV)
            qg = jnp.concatenate(
                [q_ref[rows, hd * HEAD_DIM:(hd + 1) * HEAD_DIM] for hd in heads], axis=0)
            kwin = kv_scr[win, g * HEAD_DIM:(g + 1) * HEAD_DIM]
            vwin = kv_scr[win, KV_WIDTH + g * HEAD_DIM:KV_WIDTH + (g + 1) * HEAD_DIM]
            s = lax.dot_general(qg, kwin, (((1,), (1,)), ((), ())),
                                preferred_element_type=F32) * scale
            probs, inv_l = [], []
            for hh, hd in enumerate(heads):
                sink = sink_ref[layer, hd]
                sh = s[hh * BLOCK:(hh + 1) * BLOCK, :] + bias
                m = jnp.maximum(jnp.max(sh, axis=-1, keepdims=True), sink)
                p = jnp.exp(sh - m)
                denom = jnp.sum(p, axis=-1, keepdims=True) + jnp.exp(sink - m)
                probs.append(p.astype(BF16))
                inv_l.append(1.0 / denom)
            o = jnp.dot(jnp.concatenate(probs, axis=0), vwin, preferred_element_type=F32)
            for hh, hd in enumerate(heads):
                oh = o[hh * BLOCK:(hh + 1) * BLOCK, :] * inv_l[hh]
                o_ref[rows, hd * HEAD_DIM:(hd + 1) * HEAD_DIM] = oh.astype(BF16)

    halo = cp_ref.shape[0]
    c_scr[0, 0:halo, :] = cp_ref[...].astype(F32) * (1 - is_first).astype(F32)
    c_scr[0, halo:halo + tb, :] = cc_ref[...].astype(F32)
    c_scr[0, halo + tb:, :] = cn_ref[...].astype(F32) * (1 - is_last).astype(F32)
    shifted_rows = tb + 2 * halo - F32_SUBLANE_TILE
    for s in range(1, F32_SUBLANE_TILE):
        c_scr[s, 0:shifted_rows, :] = c_scr[0, s:s + shifted_rows, :]
    first_tap = halo - CONV_PAD

    def conv_rows(t, carry):
        r0 = pl.multiple_of(t * CONV_ROWS, CONV_ROWS)
        acc = jnp.broadcast_to(dwb_ref[...], (CONV_ROWS, CONV_WIDTH))
        for k in range(CONV_KERNEL):
            shift = (first_tap + k) % F32_SUBLANE_TILE
            start = pl.multiple_of(r0 + (first_tap + k - shift), F32_SUBLANE_TILE)
            acc = acc + c_scr[shift, pl.ds(start, CONV_ROWS), :] * dww_ref[k:k + 1, :]
        y = _layer_norm_rows(acc, clg_ref[...], clb_ref[...])
        o_ref[pl.ds(r0, CONV_ROWS), ATTN_WIDTH:ATTN_WIDTH + CONV_WIDTH] = (
            y * jax.nn.sigmoid(y)).astype(BF16)
        return carry

    lax.fori_loop(0, tb // CONV_ROWS, conv_rows, 0)

    sgu_col = ATTN_WIDTH + CONV_WIDTH
    for ci in range(tb // CHUNK):
        rows = slice(ci * CHUNK, (ci + 1) * CHUNK)
        v = _layer_norm_rows(gu_ref[rows, SGU_WIDTH:].astype(F32), slg_ref[...], slb_ref[...])
        v = v.astype(BF16)
        for hd in range(SGU_HEADS):
            cols = slice(hd * HEAD_DIM, (hd + 1) * HEAD_DIM)
            sp = jnp.dot(sw_ref[hd], v[:, cols], preferred_element_type=F32) + sbt_ref[:, hd:hd + 1]
            u = gu_ref[rows, cols].astype(F32)
            o_ref[rows, sgu_col + hd * HEAD_DIM:sgu_col + (hd + 1) * HEAD_DIM] = (u * sp).astype(BF16)


def _mixer(layer, sink, q, kv, c, gu, dww, dwb, clg, clb, slg, slb, sgu_w, sgu_bt):
    batch, seq, _ = q.shape
    tb = TB_MIX
    halo = BF16_SUBLANE_TILE
    assert CONV_PAD <= halo and tb % BLOCK == 0 and tb % CONV_ROWS == 0
    kv_per = tb // BLOCK
    c_per = tb // halo
    cur = lambda b, i: (b, i, 0)
    kv_prev = lambda b, i: (b, jnp.maximum(i * kv_per - 1, 0), 0)
    kv_next = lambda b, i: (b, jnp.minimum((i + 1) * kv_per, seq // BLOCK - 1), 0)
    c_prev = lambda b, i: (b, jnp.maximum(i * c_per - 1, 0), 0)
    c_next = lambda b, i: (b, jnp.minimum((i + 1) * c_per, seq // halo - 1), 0)
    per_layer = lambda b, i: (layer, 0, 0)
    vec = lambda w: pl.BlockSpec((None, 1, w), per_layer)
    return pl.pallas_call(
        functools.partial(_mixer_kernel, layer),
        grid=(batch, seq // tb),
        in_specs=[
            pl.BlockSpec(memory_space=pltpu.SMEM),
            pl.BlockSpec((None, tb, ATTN_WIDTH), cur),
            pl.BlockSpec((None, tb, 2 * KV_WIDTH), cur),
            pl.BlockSpec((None, BLOCK, 2 * KV_WIDTH), kv_prev),
            pl.BlockSpec((None, BLOCK, 2 * KV_WIDTH), kv_next),
            pl.BlockSpec((None, tb, CONV_WIDTH), cur),
            pl.BlockSpec((None, halo, CONV_WIDTH), c_prev),
            pl.BlockSpec((None, halo, CONV_WIDTH), c_next),
            pl.BlockSpec((None, tb, 2 * SGU_WIDTH), cur),
            pl.BlockSpec((None, CONV_KERNEL, CONV_WIDTH), per_layer),
            vec(CONV_WIDTH), vec(CONV_WIDTH), vec(CONV_WIDTH), vec(SGU_WIDTH), vec(SGU_WIDTH),
            pl.BlockSpec((None, SGU_HEADS, CHUNK, CHUNK), lambda b, i: (layer, 0, 0, 0)),
            pl.BlockSpec((None, CHUNK, SGU_HEADS), per_layer),
        ],
        out_specs=pl.BlockSpec((None, tb, D_MODEL), cur),
        out_shape=jax.ShapeDtypeStruct((batch, seq, D_MODEL), BF16),
        scratch_shapes=[
            pltpu.VMEM((tb + 2 * BLOCK, 2 * KV_WIDTH), BF16),
            pltpu.VMEM((F32_SUBLANE_TILE, tb + 2 * halo, CONV_WIDTH), F32),
        ],
        compiler_params=pltpu.CompilerParams(
            dimension_semantics=("arbitrary", "arbitrary"), vmem_limit_bytes=V7X_VMEM_LIMIT_BYTES),
        name=f"mixer_l{layer}",
    )(sink, q, kv, kv, kv, c, c, c, gu, dww, dwb, clg, clb, slg, slb, sgu_w, sgu_bt)


def _out_ffn_kernel(apply_final_norm, x_ref, mix_ref, wo_ref, g_ref, wg_ref, wu_ref, wd_ref, fg_ref,
                    o_ref, h_scr):
    f = pl.program_id(1)

    @pl.when(f == 0)
    def _():
        xn = x_ref[...] + jnp.dot(mix_ref[...], wo_ref[...], preferred_element_type=F32)
        o_ref[...] = xn
        h_scr[...] = _rms_norm_rows(xn, g_ref[...]).astype(BF16)

    h = h_scr[...]
    gate = jnp.dot(h, wg_ref[...], preferred_element_type=F32)
    up = jnp.dot(h, wu_ref[...], preferred_element_type=F32)
    act = (gate * jax.nn.sigmoid(gate) * up).astype(BF16)
    o_ref[...] += jnp.dot(act, wd_ref[...], preferred_element_type=F32)

    if apply_final_norm:
        @pl.when(f == pl.num_programs(1) - 1)
        def _():
            o_ref[...] = _rms_norm_rows(o_ref[...], fg_ref[...])


def _out_ffn(layer, x, mix, w_out, norm_g, w_gate, w_up, w_down, final_g, apply_final_norm):
    tokens = x.shape[0]
    tm, tf = TM_FFN, TF_FFN
    row = lambda i, f: (i, 0)
    return pl.pallas_call(
        functools.partial(_out_ffn_kernel, apply_final_norm),
        grid=(tokens // tm, D_FF // tf),
        in_specs=[
            pl.BlockSpec((tm, D_MODEL), row),
            pl.BlockSpec((tm, D_MODEL), row),
            pl.BlockSpec((None, D_MODEL, D_MODEL), lambda i, f: (layer, 0, 0),
                         pipeline_mode=pl.Buffered(1)),
            pl.BlockSpec((None, 1, D_MODEL), lambda i, f: (layer, 0, 0)),
            pl.BlockSpec((None, D_MODEL, tf), lambda i, f: (layer, 0, f)),
            pl.BlockSpec((None, D_MODEL, tf), lambda i, f: (layer, 0, f)),
            pl.BlockSpec((None, tf, D_MODEL), lambda i, f: (layer, f, 0)),
            pl.BlockSpec((1, D_MODEL), lambda i, f: (0, 0)),
        ],
        out_specs=pl.BlockSpec((tm, D_MODEL), row),
        out_shape=jax.ShapeDtypeStruct((tokens, D_MODEL), F32),
        scratch_shapes=[pltpu.VMEM((tm, D_MODEL), BF16)],
        compiler_params=pltpu.CompilerParams(
            dimension_semantics=("arbitrary", "arbitrary"), vmem_limit_bytes=V7X_VMEM_LIMIT_BYTES),
        name=f"out_ffn_l{layer}",
    )(x, mix, w_out, norm_g, w_gate, w_up, w_down, final_g)


@jax.jit
def _forward(x, mix_norm_g, w_in, sink, conv_dw_w, conv_dw_b, conv_ln_g, conv_ln_b,
             sgu_ln_g, sgu_ln_b, sgu_w, sgu_b, w_out, ffn_norm_g, w_gate, w_up, w_down,
             final_norm_g):
    batch, seq, d = x.shape
    tokens = batch * seq
    ropes = _rope_tables(seq)
    w_in, w_out, w_gate, w_up, w_down, sgu_w = (
        w.astype(BF16) for w in (w_in, w_out, w_gate, w_up, w_down, sgu_w))
    vec3 = lambda p: p.reshape(DEPTH, 1, p.shape[-1])
    mix_norm_g, ffn_norm_g = vec3(mix_norm_g), vec3(ffn_norm_g)
    conv_dw_b, conv_ln_g, conv_ln_b = vec3(conv_dw_b), vec3(conv_ln_g), vec3(conv_ln_b)
    sgu_ln_g, sgu_ln_b = vec3(sgu_ln_g), vec3(sgu_ln_b)
    sgu_bt = jnp.swapaxes(sgu_b, 1, 2)
    final_g = final_norm_g.reshape(1, d)

    xf = x.reshape(tokens, d)
    for layer in range(DEPTH):
        q, kv, c, gu = _in_proj(layer, xf, mix_norm_g, ropes, w_in, seq)
        to3 = lambda a: a.reshape(batch, seq, a.shape[-1])
        mix = _mixer(layer, sink, to3(q), to3(kv), to3(c), to3(gu), conv_dw_w, conv_dw_b,
                     conv_ln_g, conv_ln_b, sgu_ln_g, sgu_ln_b, sgu_w, sgu_bt)
        xf = _out_ffn(layer, xf, mix.reshape(tokens, d), w_out, ffn_norm_g, w_gate, w_up, w_down,
                      final_g, apply_final_norm=(layer == DEPTH - 1))
    return xf.reshape(batch, seq, d)


def kernel(x, mix_norm_g, w_in, sink, conv_dw_w, conv_dw_b, conv_ln_g, conv_ln_b, sgu_ln_g, sgu_ln_b,
           sgu_w, sgu_b, w_out, ffn_norm_g, w_gate, w_up, w_down, final_norm_g):
    return _forward(x, mix_norm_g, w_in, sink, conv_dw_w, conv_dw_b, conv_ln_g, conv_ln_b,
                    sgu_ln_g, sgu_ln_b, sgu_w, sgu_b, w_out, ffn_norm_g, w_gate, w_up, w_down,
                    final_norm_g)
```

```python
import functools

import jax
import jax.numpy as jnp
import numpy as np
from jax import lax
from jax.experimental import pallas as pl
from jax.experimental.pallas import tpu as pltpu

F32 = jnp.float32
BF16 = jnp.bfloat16

D_MODEL = 2048
DEPTH = 4
HEAD_DIM = 128
ATTN_WIDTH = D_MODEL // 2
N_Q_HEADS = ATTN_WIDTH // HEAD_DIM
N_KV_HEADS = N_Q_HEADS // 4
Q_PER_KV = N_Q_HEADS // N_KV_HEADS
KV_WIDTH = N_KV_HEADS * HEAD_DIM
CONV_WIDTH = D_MODEL // 4
CONV_KERNEL = 31
CONV_PAD = (CONV_KERNEL - 1) // 2
SGU_WIDTH = D_MODEL // 4
SGU_HEADS = SGU_WIDTH // HEAD_DIM
CHUNK = 128
IN_WIDTH = ATTN_WIDTH + 2 * KV_WIDTH + 2 * CONV_WIDTH + 2 * SGU_WIDTH
WINDOW = 128
BLOCK = 128
ROPE_THETA = 500000.0
ROT_DIM = HEAD_DIM // 4
ROT_HALF = ROT_DIM // 2
D_FF = ((8 * D_MODEL // 3 + 255) // 256) * 256
EPS = 1e-6

K_START = ATTN_WIDTH
V_START = K_START + KV_WIDTH
CA_START = V_START + KV_WIDTH
CG_START = CA_START + CONV_WIDTH
UV_START = CG_START + CONV_WIDTH

V7X_VMEM_LIMIT_BYTES = 60 * 1024 * 1024
BF16_SUBLANE_TILE = 16
F32_SUBLANE_TILE = 8
TM_PROJ = 512
TB_MIX = 512
TM_FFN = 512
TF_FFN = 512
CONV_ROWS = 32
MASK_VALUE = -1e30
LOG2_E = np.float32(np.log2(np.e))
SCORE_SCALE_LOG2 = np.float32(np.log2(np.e) / np.sqrt(HEAD_DIM))


def _rms_norm_rows(x, g):
    ms = jnp.mean(x * x, axis=-1, keepdims=True)
    return x * lax.rsqrt(ms + EPS) * g


def _layer_norm_rows(x, g, b):
    mu = jnp.mean(x, axis=-1, keepdims=True)
    xc = x - mu
    var = jnp.mean(xc * xc, axis=-1, keepdims=True)
    return xc * lax.rsqrt(var + EPS) * g + b


def _rope_tables(seq):
    pos = jnp.arange(seq, dtype=F32)
    inv = ROPE_THETA ** (-jnp.arange(0, ROT_DIM, 2, dtype=F32) / ROT_DIM)
    ang = pos[:, None] * inv[None, :]
    cos, sin = jnp.cos(ang), jnp.sin(ang)
    rest = HEAD_DIM - ROT_DIM
    cos_t = jnp.concatenate([cos, cos, jnp.ones((seq, rest), F32)], axis=-1)
    sin_a = jnp.concatenate([-sin, jnp.zeros((seq, HEAD_DIM - ROT_HALF), F32)], axis=-1)
    sin_b = jnp.concatenate([jnp.zeros((seq, ROT_HALF), F32), sin, jnp.zeros((seq, rest), F32)], axis=-1)
    return cos_t, sin_a, sin_b


def _in_proj_kernel(x_ref, g_ref, cos_ref, sa_ref, sb_ref, w_ref, q_ref, kv_ref, c_ref, gu_ref, h_scr):
    h_scr[...] = _rms_norm_rows(x_ref[...], g_ref[...]).astype(BF16)
    cos, sin_a, sin_b = cos_ref[...], sa_ref[...], sb_ref[...]

    def rope(t):
        return (t * cos + pltpu.roll(t, HEAD_DIM - ROT_HALF, 1) * sin_a
                + pltpu.roll(t, ROT_HALF, 1) * sin_b)

    def proj(lo, hi):
        return jnp.dot(h_scr[...], w_ref[:, lo:hi], preferred_element_type=F32)

    zq = proj(0, K_START)
    for hd in range(N_Q_HEADS):
        sl = slice(hd * HEAD_DIM, (hd + 1) * HEAD_DIM)
        q_ref[:, sl] = (rope(zq[:, sl]) * SCORE_SCALE_LOG2).astype(BF16)
    zk = proj(K_START, V_START)
    for hd in range(N_KV_HEADS):
        sl = slice(hd * HEAD_DIM, (hd + 1) * HEAD_DIM)
        kv_ref[:, sl] = rope(zk[:, sl]).astype(BF16)
    kv_ref[:, KV_WIDTH:] = proj(V_START, CA_START).astype(BF16)
    za = proj(CA_START, CG_START)
    zg = proj(CG_START, UV_START)
    c_ref[...] = (za * jax.nn.sigmoid(zg)).astype(BF16)
    zu = proj(UV_START, IN_WIDTH)
    gelu = 0.5 * zu * (1.0 + lax.erf(zu * np.float32(np.sqrt(0.5))))
    gu_ref[...] = gelu.astype(BF16)


def _in_proj(layer, x, norm_g, ropes, w_in, seq):
    tokens = x.shape[0]
    tm = TM_PROJ
    seq_blocks = seq // tm
    row = lambda i: (i, 0)
    rope_spec = pl.BlockSpec((tm, HEAD_DIM), lambda i: (i % seq_blocks, 0))
    out_w = (ATTN_WIDTH, 2 * KV_WIDTH, CONV_WIDTH, 2 * SGU_WIDTH)
    return pl.pallas_call(
        _in_proj_kernel,
        grid=(tokens // tm,),
        in_specs=[
            pl.BlockSpec((tm, D_MODEL), row),
            pl.BlockSpec((None, 1, D_MODEL), lambda i: (layer, 0, 0)),
            rope_spec, rope_spec, rope_spec,
            pl.BlockSpec((None, D_MODEL, IN_WIDTH), lambda i: (layer, 0, 0),
                         pipeline_mode=pl.Buffered(1)),
        ],
        out_specs=[pl.BlockSpec((tm, w), row) for w in out_w],
        out_shape=[jax.ShapeDtypeStruct((tokens, w), BF16) for w in out_w],
        scratch_shapes=[pltpu.VMEM((tm, D_MODEL), BF16)],
        compiler_params=pltpu.CompilerParams(
            dimension_semantics=("arbitrary",), vmem_limit_bytes=V7X_VMEM_LIMIT_BYTES),
        name=f"in_proj_l{layer}",
    )(x, norm_g, *ropes, w_in)


def _mixer_kernel(layer, sink_ref, x_ref, q_ref, kvc_ref, kvp_ref, kvn_ref, cc_ref, cp_ref, cn_ref,
                  gu_ref, dww_ref, dwb_ref, clg_ref, clb_ref, slg_ref, slb_ref, sw_ref, sbt_ref, wo_ref,
                  o_ref, mix_scr, kv_scr, c_scr, wb_scr, y_scr):
    tb = q_ref.shape[0]
    conv_col = ATTN_WIDTH
    sgu_col = ATTN_WIDTH + CONV_WIDTH

    def out_proj(lo, hi):
        return jnp.dot(mix_scr[:, lo:hi], wo_ref[lo:hi, :], preferred_element_type=F32)

    i = pl.program_id(1)
    is_first = (i == 0).astype(jnp.int32)
    is_last = (i == pl.num_programs(1) - 1).astype(jnp.int32)

    kv_scr[0:BLOCK, :] = kvp_ref[...]
    kv_scr[BLOCK:BLOCK + tb, :] = kvc_ref[...]
    kv_scr[BLOCK + tb:, :] = kvn_ref[...]
    r = lax.broadcasted_iota(jnp.int32, (BLOCK, BLOCK), 0)
    c = lax.broadcasted_iota(jnp.int32, (BLOCK, BLOCK), 1)
    prev_bias = jnp.where(c >= r, 0.0, MASK_VALUE).astype(F32)
    next_bias = jnp.where(c <= r, 0.0, MASK_VALUE).astype(F32)
    prev_bias_first = jnp.minimum(prev_bias, MASK_VALUE * is_first.astype(F32))
    next_bias_last = jnp.minimum(next_bias, MASK_VALUE * is_last.astype(F32))
    n_sub = tb // BLOCK
    for j in range(n_sub):
        lo_bias = prev_bias_first if j == 0 else prev_bias
        hi_bias = next_bias_last if j == n_sub - 1 else next_bias
        rows = slice(j * BLOCK, (j + 1) * BLOCK)
        win = slice(j * BLOCK, (j + 3) * BLOCK)
        for g in range(N_KV_HEADS):
            heads = range(g * Q_PER_KV, (g + 1) * Q_PER_KV)
            qg = jnp.concatenate(
                [q_ref[rows, hd * HEAD_DIM:(hd + 1) * HEAD_DIM] for hd in heads], axis=0)
            kwin = kv_scr[win, g * HEAD_DIM:(g + 1) * HEAD_DIM]
            vwin = kv_scr[win, KV_WIDTH + g * HEAD_DIM:KV_WIDTH + (g + 1) * HEAD_DIM]
            s = lax.dot_general(qg, kwin, (((1,), (1,)), ((), ())), preferred_element_type=F32)
            probs, inv_l = [], []
            for hh, hd in enumerate(heads):
                sink = sink_ref[layer, hd] * LOG2_E
                hrows = slice(hh * BLOCK, (hh + 1) * BLOCK)
                sh = jnp.concatenate([s[hrows, 0:BLOCK] + lo_bias, s[hrows, BLOCK:2 * BLOCK],
                                      s[hrows, 2 * BLOCK:] + hi_bias], axis=1)
                m = jnp.maximum(jnp.max(sh, axis=-1, keepdims=True), sink)
                p = jnp.exp2(sh - m)
                denom = jnp.sum(p, axis=-1, keepdims=True) + jnp.exp2(sink - m)
                probs.append(p.astype(BF16))
                inv_l.append(1.0 / denom)
            o = jnp.dot(jnp.concatenate(probs, axis=0), vwin, preferred_element_type=F32)
            for hh, hd in enumerate(heads):
                oh = o[hh * BLOCK:(hh + 1) * BLOCK, :] * inv_l[hh]
                mix_scr[rows, hd * HEAD_DIM:(hd + 1) * HEAD_DIM] = oh.astype(BF16)
    o_ref[...] = x_ref[...] + out_proj(0, conv_col)

    for ci in range(tb // CHUNK):
        rows = slice(ci * CHUNK, (ci + 1) * CHUNK)
        v = _layer_norm_rows(gu_ref[rows, SGU_WIDTH:].astype(F32), slg_ref[...], slb_ref[...])
        v = v.astype(BF16)
        for hd in range(SGU_HEADS):
            cols = slice(hd * HEAD_DIM, (hd + 1) * HEAD_DIM)
            sp = jnp.dot(sw_ref[hd], v[:, cols], preferred_element_type=F32) + sbt_ref[:, hd:hd + 1]
            u = gu_ref[rows, cols].astype(F32)
            mix_scr[rows, sgu_col + hd * HEAD_DIM:sgu_col + (hd + 1) * HEAD_DIM] = (u * sp).astype(BF16)
    o_ref[...] += out_proj(sgu_col, D_MODEL)

    halo = cp_ref.shape[0]
    c_scr[0, 0:halo, :] = cp_ref[...].astype(F32) * (1 - is_first).astype(F32)
    c_scr[0, halo:halo + tb, :] = cc_ref[...].astype(F32)
    c_scr[0, halo + tb:, :] = cn_ref[...].astype(F32) * (1 - is_last).astype(F32)
    shifted_rows = tb + 2 * halo - F32_SUBLANE_TILE
    for s in range(1, F32_SUBLANE_TILE):
        c_scr[s, 0:shifted_rows, :] = c_scr[0, s:s + shifted_rows, :]
    first_tap = halo - CONV_PAD

    for k in range(CONV_KERNEL):
        wb_scr[k] = jnp.broadcast_to(dww_ref[k:k + 1, :], (F32_SUBLANE_TILE, CONV_WIDTH))
    wb_scr[CONV_KERNEL] = jnp.broadcast_to(dwb_ref[...], (F32_SUBLANE_TILE, CONV_WIDTH))
    groups = CONV_ROWS // F32_SUBLANE_TILE

    def conv_rows(t, carry):
        r0 = pl.multiple_of(t * CONV_ROWS, CONV_ROWS)
        acc = [wb_scr[CONV_KERNEL]] * groups
        for k in range(CONV_KERNEL):
            shift = (first_tap + k) % F32_SUBLANE_TILE
            w = wb_scr[k]
            for gi in range(groups):
                start = pl.multiple_of(
                    r0 + (first_tap + k - shift + gi * F32_SUBLANE_TILE), F32_SUBLANE_TILE)
                acc[gi] = acc[gi] + c_scr[shift, pl.ds(start, F32_SUBLANE_TILE), :] * w
        for gi in range(groups):
            out_row = pl.multiple_of(r0 + gi * F32_SUBLANE_TILE, F32_SUBLANE_TILE)
            y_scr[pl.ds(out_row, F32_SUBLANE_TILE), :] = acc[gi]
        return carry

    lax.fori_loop(0, tb // CONV_ROWS, conv_rows, 0)
    y = _layer_norm_rows(y_scr[...], clg_ref[...], clb_ref[...])
    mix_scr[:, conv_col:sgu_col] = (y * jax.nn.sigmoid(y)).astype(BF16)
    o_ref[...] += out_proj(conv_col, sgu_col)


def _mixer(layer, sink, x, q, kv, c, gu, dww, dwb, clg, clb, slg, slb, sgu_w, sgu_bt, w_out):
    batch, seq, _ = q.shape
    tb = TB_MIX
    halo = BF16_SUBLANE_TILE
    assert CONV_PAD <= halo and tb % BLOCK == 0 and tb % CONV_ROWS == 0
    kv_per = tb // BLOCK
    c_per = tb // halo
    cur = lambda b, i: (b, i, 0)
    kv_prev = lambda b, i: (b, jnp.maximum(i * kv_per - 1, 0), 0)
    kv_next = lambda b, i: (b, jnp.minimum((i + 1) * kv_per, seq // BLOCK - 1), 0)
    c_prev = lambda b, i: (b, jnp.maximum(i * c_per - 1, 0), 0)
    c_next = lambda b, i: (b, jnp.minimum((i + 1) * c_per, seq // halo - 1), 0)
    per_layer = lambda b, i: (layer, 0, 0)
    vec = lambda w: pl.BlockSpec((None, 1, w), per_layer)
    return pl.pallas_call(
        functools.partial(_mixer_kernel, layer),
        grid=(batch, seq // tb),
        in_specs=[
            pl.BlockSpec(memory_space=pltpu.SMEM),
            pl.BlockSpec((None, tb, D_MODEL), cur),
            pl.BlockSpec((None, tb, ATTN_WIDTH), cur),
            pl.BlockSpec((None, tb, 2 * KV_WIDTH), cur),
            pl.BlockSpec((None, BLOCK, 2 * KV_WIDTH), kv_prev),
            pl.BlockSpec((None, BLOCK, 2 * KV_WIDTH), kv_next),
            pl.BlockSpec((None, tb, CONV_WIDTH), cur),
            pl.BlockSpec((None, halo, CONV_WIDTH), c_prev),
            pl.BlockSpec((None, halo, CONV_WIDTH), c_next),
            pl.BlockSpec((None, tb, 2 * SGU_WIDTH), cur),
            pl.BlockSpec((None, CONV_KERNEL, CONV_WIDTH), per_layer),
            vec(CONV_WIDTH), vec(CONV_WIDTH), vec(CONV_WIDTH), vec(SGU_WIDTH), vec(SGU_WIDTH),
            pl.BlockSpec((None, SGU_HEADS, CHUNK, CHUNK), lambda b, i: (layer, 0, 0, 0)),
            pl.BlockSpec((None, CHUNK, SGU_HEADS), per_layer),
            pl.BlockSpec((None, D_MODEL, D_MODEL), per_layer, pipeline_mode=pl.Buffered(1)),
        ],
        out_specs=pl.BlockSpec((None, tb, D_MODEL), cur),
        out_shape=jax.ShapeDtypeStruct((batch, seq, D_MODEL), F32),
        scratch_shapes=[
            pltpu.VMEM((tb, D_MODEL), BF16),
            pltpu.VMEM((tb + 2 * BLOCK, 2 * KV_WIDTH), BF16),
            pltpu.VMEM((F32_SUBLANE_TILE, tb + 2 * halo, CONV_WIDTH), F32),
            pltpu.VMEM((CONV_KERNEL + 1, F32_SUBLANE_TILE, CONV_WIDTH), F32),
            pltpu.VMEM((tb, CONV_WIDTH), F32),
        ],
        compiler_params=pltpu.CompilerParams(
            dimension_semantics=("arbitrary", "arbitrary"), vmem_limit_bytes=V7X_VMEM_LIMIT_BYTES),
        name=f"mixer_l{layer}",
    )(sink, x, q, kv, kv, kv, c, c, c, gu, dww, dwb, clg, clb, slg, slb, sgu_w, sgu_bt, w_out)


def _ffn_kernel(apply_final_norm, x_ref, g_ref, wg_ref, wu_ref, wd_ref, fg_ref, o_ref, h_scr):
    f = pl.program_id(1)

    @pl.when(f == 0)
    def _():
        x = x_ref[...]
        o_ref[...] = x
        h_scr[...] = _rms_norm_rows(x, g_ref[...]).astype(BF16)

    h = h_scr[...]
    gate = jnp.dot(h, wg_ref[...], preferred_element_type=F32)
    up = jnp.dot(h, wu_ref[...], preferred_element_type=F32)
    act = (gate * jax.nn.sigmoid(gate) * up).astype(BF16)
    o_ref[...] += jnp.dot(act, wd_ref[...], preferred_element_type=F32)

    if apply_final_norm:
        @pl.when(f == pl.num_programs(1) - 1)
        def _():
            o_ref[...] = _rms_norm_rows(o_ref[...], fg_ref[...])


def _ffn(layer, x, norm_g, w_gate, w_up, w_down, final_g, apply_final_norm):
    tokens = x.shape[0]
    tm, tf = TM_FFN, TF_FFN
    row = lambda i, f: (i, 0)
    return pl.pallas_call(
        functools.partial(_ffn_kernel, apply_final_norm),
        grid=(tokens // tm, D_FF // tf),
        in_specs=[
            pl.BlockSpec((tm, D_MODEL), row),
            pl.BlockSpec((None, 1, D_MODEL), lambda i, f: (layer, 0, 0)),
            pl.BlockSpec((None, D_MODEL, tf), lambda i, f: (layer, 0, f)),
            pl.BlockSpec((None, D_MODEL, tf), lambda i, f: (layer, 0, f)),
            pl.BlockSpec((None, tf, D_MODEL), lambda i, f: (layer, f, 0)),
            pl.BlockSpec((1, D_MODEL), lambda i, f: (0, 0)),
        ],
        out_specs=pl.BlockSpec((tm, D_MODEL), row),
        out_shape=jax.ShapeDtypeStruct((tokens, D_MODEL), F32),
        scratch_shapes=[pltpu.VMEM((tm, D_MODEL), BF16)],
        compiler_params=pltpu.CompilerParams(
            dimension_semantics=("arbitrary", "arbitrary"), vmem_limit_bytes=V7X_VMEM_LIMIT_BYTES),
        name=f"ffn_l{layer}",
    )(x, norm_g, w_gate, w_up, w_down, final_g)


@jax.jit
def _forward(x, mix_norm_g, w_in, sink, conv_dw_w, conv_dw_b, conv_ln_g, conv_ln_b,
             sgu_ln_g, sgu_ln_b, sgu_w, sgu_b, w_out, ffn_norm_g, w_gate, w_up, w_down,
             final_norm_g):
    batch, seq, d = x.shape
    tokens = batch * seq
    ropes = _rope_tables(seq)
    w_in, w_out, w_gate, w_up, w_down, sgu_w = (
        w.astype(BF16) for w in (w_in, w_out, w_gate, w_up, w_down, sgu_w))
    vec3 = lambda p: p.reshape(DEPTH, 1, p.shape[-1])
    mix_norm_g, ffn_norm_g = vec3(mix_norm_g), vec3(ffn_norm_g)
    conv_dw_b, conv_ln_g, conv_ln_b = vec3(conv_dw_b), vec3(conv_ln_g), vec3(conv_ln_b)
    sgu_ln_g, sgu_ln_b = vec3(sgu_ln_g), vec3(sgu_ln_b)
    sgu_bt = jnp.swapaxes(sgu_b, 1, 2)
    final_g = final_norm_g.reshape(1, d)

    xf = x.reshape(tokens, d)
    for layer in range(DEPTH):
        q, kv, c, gu = _in_proj(layer, xf, mix_norm_g, ropes, w_in, seq)
        to3 = lambda a: a.reshape(batch, seq, a.shape[-1])
        xn = _mixer(layer, sink, to3(xf), to3(q), to3(kv), to3(c), to3(gu), conv_dw_w, conv_dw_b,
                    conv_ln_g, conv_ln_b, sgu_ln_g, sgu_ln_b, sgu_w, sgu_bt, w_out)
        xf = _ffn(layer, xn.reshape(tokens, d), ffn_norm_g, w_gate, w_up, w_down, final_g,
                  apply_final_norm=(layer == DEPTH - 1))
    return xf.reshape(batch, seq, d)


def kernel(x, mix_norm_g, w_in, sink, conv_dw_w, conv_dw_b, conv_ln_g, conv_ln_b, sgu_ln_g, sgu_ln_b,
           sgu_w, sgu_b, w_out, ffn_norm_g, w_gate, w_up, w_down, final_norm_g):
    return _forward(x, mix_norm_g, w_in, sink, conv_dw_w, conv_dw_b, conv_ln_g, conv_ln_b,
                    sgu_ln_g, sgu_ln_b, sgu_w, sgu_b, w_out, ffn_norm_g, w_gate, w_up, w_down,
                    final_norm_g)
```

```python
import functools

import jax
import jax.numpy as jnp
import numpy as np
from jax import lax
from jax.experimental import pallas as pl
from jax.experimental.pallas import tpu as pltpu

F32 = jnp.float32
BF16 = jnp.bfloat16

D_MODEL = 2048
DEPTH = 4
HEAD_DIM = 128
ATTN_WIDTH = D_MODEL // 2
N_Q_HEADS = ATTN_WIDTH // HEAD_DIM
N_KV_HEADS = N_Q_HEADS // 4
Q_PER_KV = N_Q_HEADS // N_KV_HEADS
KV_WIDTH = N_KV_HEADS * HEAD_DIM
CONV_WIDTH = D_MODEL // 4
CONV_KERNEL = 31
CONV_PAD = (CONV_KERNEL - 1) // 2
SGU_WIDTH = D_MODEL // 4
SGU_HEADS = SGU_WIDTH // HEAD_DIM
CHUNK = 128
IN_WIDTH = ATTN_WIDTH + 2 * KV_WIDTH + 2 * CONV_WIDTH + 2 * SGU_WIDTH
WINDOW = 128
BLOCK = 128
ROPE_THETA = 500000.0
ROT_DIM = HEAD_DIM // 4
ROT_HALF = ROT_DIM // 2
D_FF = ((8 * D_MODEL // 3 + 255) // 256) * 256
EPS = 1e-6

K_START = ATTN_WIDTH
V_START = K_START + KV_WIDTH
CA_START = V_START + KV_WIDTH
CG_START = CA_START + CONV_WIDTH
UV_START = CG_START + CONV_WIDTH

V7X_VMEM_LIMIT_BYTES = 60 * 1024 * 1024
BF16_SUBLANE_TILE = 16
F32_SUBLANE_TILE = 8
TM_PROJ = 512
TB_MIX = 512
TM_FFN = 512
TF_FFN = 1024
FFN_SLOTS = 2
CONV_ROWS = 64
CONV_UNROLL = 4
MASK_VALUE = -1e30
LOG2_E = np.float32(np.log2(np.e))
SCORE_SCALE_LOG2 = np.float32(np.log2(np.e) / np.sqrt(HEAD_DIM))


def _rms_norm_rows(x, g):
    ms = jnp.mean(x * x, axis=-1, keepdims=True)
    return x * lax.rsqrt(ms + EPS) * g


def _layer_norm_rows(x, g, b):
    mu = jnp.mean(x, axis=-1, keepdims=True)
    xc = x - mu
    var = jnp.mean(xc * xc, axis=-1, keepdims=True)
    return xc * lax.rsqrt(var + EPS) * g + b


def _rope_tables(seq):
    pos = jnp.arange(seq, dtype=F32)
    inv = ROPE_THETA ** (-jnp.arange(0, ROT_DIM, 2, dtype=F32) / ROT_DIM)
    ang = pos[:, None] * inv[None, :]
    cos, sin = jnp.cos(ang), jnp.sin(ang)
    rest = HEAD_DIM - ROT_DIM
    cos_t = jnp.concatenate([cos, cos, jnp.ones((seq, rest), F32)], axis=-1)
    sin_a = jnp.concatenate([-sin, jnp.zeros((seq, HEAD_DIM - ROT_HALF), F32)], axis=-1)
    sin_b = jnp.concatenate([jnp.zeros((seq, ROT_HALF), F32), sin, jnp.zeros((seq, rest), F32)], axis=-1)
    return cos_t, sin_a, sin_b


def _in_proj_kernel(x_ref, g_ref, cos_ref, sa_ref, sb_ref, w_ref, q_ref, kv_ref, c_ref, gu_ref, h_scr):
    h_scr[...] = _rms_norm_rows(x_ref[...], g_ref[...]).astype(BF16)
    cos, sin_a, sin_b = cos_ref[...], sa_ref[...], sb_ref[...]

    def rope(t):
        return (t * cos + pltpu.roll(t, HEAD_DIM - ROT_HALF, 1) * sin_a
                + pltpu.roll(t, ROT_HALF, 1) * sin_b)

    def proj(lo, hi):
        return jnp.dot(h_scr[...], w_ref[:, lo:hi], preferred_element_type=F32)

    zq = proj(0, K_START)
    for hd in range(N_Q_HEADS):
        sl = slice(hd * HEAD_DIM, (hd + 1) * HEAD_DIM)
        q_ref[:, sl] = (rope(zq[:, sl]) * SCORE_SCALE_LOG2).astype(BF16)
    zk = proj(K_START, V_START)
    for hd in range(N_KV_HEADS):
        sl = slice(hd * HEAD_DIM, (hd + 1) * HEAD_DIM)
        kv_ref[:, sl] = rope(zk[:, sl]).astype(BF16)
    kv_ref[:, KV_WIDTH:] = proj(V_START, CA_START).astype(BF16)
    za = proj(CA_START, CG_START)
    zg = proj(CG_START, UV_START)
    c_ref[...] = (za * jax.nn.sigmoid(zg)).astype(BF16)
    zu = proj(UV_START, IN_WIDTH)
    gelu = 0.5 * zu * (1.0 + lax.erf(zu * np.float32(np.sqrt(0.5))))
    gu_ref[...] = gelu.astype(BF16)


def _in_proj(layer, x, norm_g, ropes, w_in, seq):
    tokens = x.shape[0]
    tm = TM_PROJ
    seq_blocks = seq // tm
    row = lambda i: (i, 0)
    rope_spec = pl.BlockSpec((tm, HEAD_DIM), lambda i: (i % seq_blocks, 0))
    out_w = (ATTN_WIDTH, 2 * KV_WIDTH, CONV_WIDTH, 2 * SGU_WIDTH)
    return pl.pallas_call(
        _in_proj_kernel,
        grid=(tokens // tm,),
        in_specs=[
            pl.BlockSpec((tm, D_MODEL), row),
            pl.BlockSpec((None, 1, D_MODEL), lambda i: (layer, 0, 0)),
            rope_spec, rope_spec, rope_spec,
            pl.BlockSpec((None, D_MODEL, IN_WIDTH), lambda i: (layer, 0, 0),
                         pipeline_mode=pl.Buffered(1)),
        ],
        out_specs=[pl.BlockSpec((tm, w), row) for w in out_w],
        out_shape=[jax.ShapeDtypeStruct((tokens, w), BF16) for w in out_w],
        scratch_shapes=[pltpu.VMEM((tm, D_MODEL), BF16)],
        compiler_params=pltpu.CompilerParams(
            dimension_semantics=("arbitrary",), vmem_limit_bytes=V7X_VMEM_LIMIT_BYTES),
        name=f"in_proj_l{layer}",
    )(x, norm_g, *ropes, w_in)


def _mixer_kernel(layer, sink_ref, x_ref, q_ref, kvc_ref, kvp_ref, kvn_ref, cc_ref, cp_ref, cn_ref,
                  gu_ref, dww_ref, dwb_ref, clg_ref, clb_ref, slg_ref, slb_ref, sw_ref, sbt_ref, wo_ref,
                  o_ref, mix_scr, kv_scr, c_scr, wb_scr, y_scr, proj_scr):
    tb = q_ref.shape[0]
    conv_col = ATTN_WIDTH
    sgu_col = ATTN_WIDTH + CONV_WIDTH

    i = pl.program_id(1)
    is_first = (i == 0).astype(jnp.int32)
    is_last = (i == pl.num_programs(1) - 1).astype(jnp.int32)

    kv_scr[0:BLOCK, :] = kvp_ref[...]
    kv_scr[BLOCK:BLOCK + tb, :] = kvc_ref[...]
    kv_scr[BLOCK + tb:, :] = kvn_ref[...]
    r = lax.broadcasted_iota(jnp.int32, (BLOCK, BLOCK), 0)
    c = lax.broadcasted_iota(jnp.int32, (BLOCK, BLOCK), 1)
    prev_bias = jnp.where(c >= r, 0.0, MASK_VALUE).astype(F32)
    next_bias = jnp.where(c <= r, 0.0, MASK_VALUE).astype(F32)
    prev_bias_first = jnp.minimum(prev_bias, MASK_VALUE * is_first.astype(F32))
    next_bias_last = jnp.minimum(next_bias, MASK_VALUE * is_last.astype(F32))
    n_sub = tb // BLOCK
    for j in range(n_sub):
        lo_bias = prev_bias_first if j == 0 else prev_bias
        hi_bias = next_bias_last if j == n_sub - 1 else next_bias
        rows = slice(j * BLOCK, (j + 1) * BLOCK)
        win = slice(j * BLOCK, (j + 3) * BLOCK)
        for g in range(N_KV_HEADS):
            heads = range(g * Q_PER_KV, (g + 1) * Q_PER_KV)
            qg = jnp.concatenate(
                [q_ref[rows, hd * HEAD_DIM:(hd + 1) * HEAD_DIM] for hd in heads], axis=0)
            kwin = kv_scr[win, g * HEAD_DIM:(g + 1) * HEAD_DIM]
            vwin = kv_scr[win, KV_WIDTH + g * HEAD_DIM:KV_WIDTH + (g + 1) * HEAD_DIM]
            s = lax.dot_general(qg, kwin, (((1,), (1,)), ((), ())), preferred_element_type=F32)
            probs, inv_l = [], []
            for hh, hd in enumerate(heads):
                sink = sink_ref[layer, hd] * LOG2_E
                hrows = slice(hh * BLOCK, (hh + 1) * BLOCK)
                sh = jnp.concatenate([s[hrows, 0:BLOCK] + lo_bias, s[hrows, BLOCK:2 * BLOCK],
                                      s[hrows, 2 * BLOCK:] + hi_bias], axis=1)
                m = jnp.maximum(jnp.max(sh, axis=-1, keepdims=True), sink)
                p = jnp.exp2(sh - m)
                denom = jnp.sum(p, axis=-1, keepdims=True) + jnp.exp2(sink - m)
                probs.append(p.astype(BF16))
                inv_l.append(1.0 / denom)
            o = jnp.dot(jnp.concatenate(probs, axis=0), vwin, preferred_element_type=F32)
            for hh, hd in enumerate(heads):
                oh = o[hh * BLOCK:(hh + 1) * BLOCK, :] * inv_l[hh]
                mix_scr[rows, hd * HEAD_DIM:(hd + 1) * HEAD_DIM] = oh.astype(BF16)

    for ci in range(tb // CHUNK):
        rows = slice(ci * CHUNK, (ci + 1) * CHUNK)
        v = _layer_norm_rows(gu_ref[rows, SGU_WIDTH:].astype(F32), slg_ref[...], slb_ref[...])
        v = v.astype(BF16)
        for hd in range(SGU_HEADS):
            cols = slice(hd * HEAD_DIM, (hd + 1) * HEAD_DIM)
            sp = jnp.dot(sw_ref[hd], v[:, cols], preferred_element_type=F32) + sbt_ref[:, hd:hd + 1]
            u = gu_ref[rows, cols].astype(F32)
            mix_scr[rows, sgu_col + hd * HEAD_DIM:sgu_col + (hd + 1) * HEAD_DIM] = (u * sp).astype(BF16)

    halo = cp_ref.shape[0]
    c_scr[0, 0:halo, :] = cp_ref[...].astype(F32) * (1 - is_first).astype(F32)
    c_scr[0, halo:halo + tb, :] = cc_ref[...].astype(F32)
    c_scr[0, halo + tb:, :] = cn_ref[...].astype(F32) * (1 - is_last).astype(F32)
    shifted_rows = tb + 2 * halo - F32_SUBLANE_TILE
    for s in range(1, F32_SUBLANE_TILE):
        c_scr[s, 0:shifted_rows, :] = c_scr[0, s:s + shifted_rows, :]
    first_tap = halo - CONV_PAD

    for k in range(CONV_KERNEL):
        wb_scr[k] = jnp.broadcast_to(dww_ref[k:k + 1, :], (F32_SUBLANE_TILE, CONV_WIDTH))
    wb_scr[CONV_KERNEL] = jnp.broadcast_to(dwb_ref[...], (F32_SUBLANE_TILE, CONV_WIDTH))
    groups = CONV_ROWS // F32_SUBLANE_TILE
    n_steps = wo_ref.shape[0]

    def conv_rows(t, carry):
        r0 = pl.multiple_of(t * CONV_ROWS, CONV_ROWS)
        acc = [wb_scr[CONV_KERNEL]] * groups
        for k in range(CONV_KERNEL):
            shift = (first_tap + k) % F32_SUBLANE_TILE
            w = wb_scr[k]
            for gi in range(groups):
                start = pl.multiple_of(
                    r0 + (first_tap + k - shift + gi * F32_SUBLANE_TILE), F32_SUBLANE_TILE)
                acc[gi] = acc[gi] + c_scr[shift, pl.ds(start, F32_SUBLANE_TILE), :] * w
        for gi in range(groups):
            out_row = pl.multiple_of(r0 + gi * F32_SUBLANE_TILE, F32_SUBLANE_TILE)
            y_scr[pl.ds(out_row, F32_SUBLANE_TILE), :] = acc[gi]
        proj_scr[t] = (
            jnp.dot(mix_scr[:, 0:conv_col], wo_ref[t, 0:conv_col, :], preferred_element_type=F32)
            + jnp.dot(mix_scr[:, sgu_col:], wo_ref[t, sgu_col:, :], preferred_element_type=F32))
        return carry

    lax.fori_loop(0, n_steps, conv_rows, 0, unroll=CONV_UNROLL)
    y = _layer_norm_rows(y_scr[...], clg_ref[...], clb_ref[...])
    conv_feat = (y * jax.nn.sigmoid(y)).astype(BF16)
    out_cols = wo_ref.shape[2]
    for t in range(n_steps):
        cols = slice(t * out_cols, (t + 1) * out_cols)
        o_ref[:, cols] = x_ref[:, cols] + proj_scr[t] + jnp.dot(
            conv_feat, wo_ref[t, conv_col:sgu_col, :], preferred_element_type=F32)


def _mixer(layer, sink, x, q, kv, c, gu, dww, dwb, clg, clb, slg, slb, sgu_w, sgu_bt, w_out):
    batch, seq, _ = q.shape
    tb = TB_MIX
    halo = BF16_SUBLANE_TILE
    assert CONV_PAD <= halo and tb % BLOCK == 0 and tb % CONV_ROWS == 0
    n_steps = tb // CONV_ROWS
    assert w_out.shape[1:] == (n_steps, D_MODEL, D_MODEL // n_steps)
    kv_per = tb // BLOCK
    c_per = tb // halo
    cur = lambda b, i: (b, i, 0)
    kv_prev = lambda b, i: (b, jnp.maximum(i * kv_per - 1, 0), 0)
    kv_next = lambda b, i: (b, jnp.minimum((i + 1) * kv_per, seq // BLOCK - 1), 0)
    c_prev = lambda b, i: (b, jnp.maximum(i * c_per - 1, 0), 0)
    c_next = lambda b, i: (b, jnp.minimum((i + 1) * c_per, seq // halo - 1), 0)
    per_layer = lambda b, i: (layer, 0, 0)
    vec = lambda w: pl.BlockSpec((None, 1, w), per_layer)
    return pl.pallas_call(
        functools.partial(_mixer_kernel, layer),
        grid=(batch, seq // tb),
        in_specs=[
            pl.BlockSpec(memory_space=pltpu.SMEM),
            pl.BlockSpec((None, tb, D_MODEL), cur),
            pl.BlockSpec((None, tb, ATTN_WIDTH), cur),
            pl.BlockSpec((None, tb, 2 * KV_WIDTH), cur),
            pl.BlockSpec((None, BLOCK, 2 * KV_WIDTH), kv_prev),
            pl.BlockSpec((None, BLOCK, 2 * KV_WIDTH), kv_next),
            pl.BlockSpec((None, tb, CONV_WIDTH), cur),
            pl.BlockSpec((None, halo, CONV_WIDTH), c_prev),
            pl.BlockSpec((None, halo, CONV_WIDTH), c_next),
            pl.BlockSpec((None, tb, 2 * SGU_WIDTH), cur),
            pl.BlockSpec((None, CONV_KERNEL, CONV_WIDTH), per_layer),
            vec(CONV_WIDTH), vec(CONV_WIDTH), vec(CONV_WIDTH), vec(SGU_WIDTH), vec(SGU_WIDTH),
            pl.BlockSpec((None, SGU_HEADS, CHUNK, CHUNK), lambda b, i: (layer, 0, 0, 0)),
            pl.BlockSpec((None, CHUNK, SGU_HEADS), per_layer),
            pl.BlockSpec((None, n_steps, D_MODEL, D_MODEL // n_steps), lambda b, i: (layer, 0, 0, 0),
                         pipeline_mode=pl.Buffered(1)),
        ],
        out_specs=pl.BlockSpec((None, tb, D_MODEL), cur),
        out_shape=jax.ShapeDtypeStruct((batch, seq, D_MODEL), F32),
        scratch_shapes=[
            pltpu.VMEM((tb, D_MODEL), BF16),
            pltpu.VMEM((tb + 2 * BLOCK, 2 * KV_WIDTH), BF16),
            pltpu.VMEM((F32_SUBLANE_TILE, tb + 2 * halo, CONV_WIDTH), F32),
            pltpu.VMEM((CONV_KERNEL + 1, F32_SUBLANE_TILE, CONV_WIDTH), F32),
            pltpu.VMEM((tb, CONV_WIDTH), F32),
            pltpu.VMEM((n_steps, tb, D_MODEL // n_steps), F32),
        ],
        compiler_params=pltpu.CompilerParams(
            dimension_semantics=("arbitrary", "arbitrary"), vmem_limit_bytes=V7X_VMEM_LIMIT_BYTES),
        name=f"mixer_l{layer}",
    )(sink, x, q, kv, kv, kv, c, c, c, gu, dww, dwb, clg, clb, slg, slb, sgu_w, sgu_bt, w_out)


def _ffn_kernel(layer, apply_final_norm, x_ref, xnext_ref, g_ref, fg_ref, wg_hbm, wu_hbm, wd_hbm,
                o_ref, h_scr, wg_buf, wu_buf, wd_buf, sems):
    i = pl.program_id(0)
    chunks = _ffn_chunks()
    n_chunks = len(chunks)
    not_last_tile = i < pl.num_programs(0) - 1

    def gate_up_copies(c):
        slot, (lo, width) = c % FFN_SLOTS, chunks[c]
        return (
            pltpu.make_async_copy(wg_hbm.at[layer, :, pl.ds(lo, width)],
                                  wg_buf.at[slot, :, pl.ds(0, width)], sems.at[0, slot]),
            pltpu.make_async_copy(wu_hbm.at[layer, :, pl.ds(lo, width)],
                                  wu_buf.at[slot, :, pl.ds(0, width)], sems.at[1, slot]),
        )

    def down_copies(c):
        slot, (lo, width) = c % FFN_SLOTS, chunks[c]
        return (
            pltpu.make_async_copy(wd_hbm.at[layer, pl.ds(lo, width), :],
                                  wd_buf.at[slot, pl.ds(0, width), :], sems.at[2, slot]),
        )

    def start(copies_of, c):
        def issue():
            for copy in copies_of(c % n_chunks):
                copy.start()
        if c < n_chunks:
            issue()
        else:
            pl.when(not_last_tile)(issue)

    def wait(copies_of, c):
        for copy in copies_of(c):
            copy.wait()

    @pl.when(i == 0)
    def _():
        start(gate_up_copies, 0)
        start(gate_up_copies, 1)
        start(down_copies, 0)
        h_scr[0] = _rms_norm_rows(x_ref[...], g_ref[...]).astype(BF16)

    cur = i % 2
    o_ref[...] = x_ref[...]
    wait(gate_up_copies, 0)

    for c, (_, width) in enumerate(chunks):
        slot = c % FFN_SLOTS
        h = h_scr[cur]
        gate = jnp.dot(h, wg_buf[slot, :, 0:width], preferred_element_type=F32)
        up = jnp.dot(h, wu_buf[slot, :, 0:width], preferred_element_type=F32)
        act = (gate * jax.nn.sigmoid(gate) * up).astype(BF16)
        if c == 1:
            h_scr[1 - cur] = _rms_norm_rows(xnext_ref[...], g_ref[...]).astype(BF16)
        start(gate_up_copies, c + 2)
        start(down_copies, c + 1)
        if c + 1 < n_chunks:
            wait(gate_up_copies, c + 1)
        wait(down_copies, c)
        o_ref[...] += jnp.dot(act, wd_buf[slot, 0:width, :], preferred_element_type=F32)

    if apply_final_norm:
        o_ref[...] = _rms_norm_rows(o_ref[...], fg_ref[...])


def _ffn_chunks():
    bounds = list(range(0, D_FF, TF_FFN)) + [D_FF]
    return tuple((lo, hi - lo) for lo, hi in zip(bounds[:-1], bounds[1:]))


def _ffn(layer, x, norm_g, w_gate, w_up, w_down, final_g, apply_final_norm):
    tokens = x.shape[0]
    tm, tf = TM_FFN, TF_FFN
    assert FFN_SLOTS == 2 and len(_ffn_chunks()) % FFN_SLOTS == 0
    row = lambda i: (i, 0)
    hbm = pl.BlockSpec(memory_space=pl.ANY)
    return pl.pallas_call(
        functools.partial(_ffn_kernel, layer, apply_final_norm),
        grid=(tokens // tm,),
        in_specs=[
            pl.BlockSpec((tm, D_MODEL), row),
            pl.BlockSpec((tm, D_MODEL), lambda i: (jnp.minimum(i + 1, tokens // tm - 1), 0)),
            pl.BlockSpec((None, 1, D_MODEL), lambda i: (layer, 0, 0)),
            pl.BlockSpec((1, D_MODEL), lambda i: (0, 0)),
            hbm, hbm, hbm,
        ],
        out_specs=pl.BlockSpec((tm, D_MODEL), row),
        out_shape=jax.ShapeDtypeStruct((tokens, D_MODEL), F32),
        scratch_shapes=[
            pltpu.VMEM((2, tm, D_MODEL), BF16),
            pltpu.VMEM((FFN_SLOTS, D_MODEL, tf), BF16),
            pltpu.VMEM((FFN_SLOTS, D_MODEL, tf), BF16),
            pltpu.VMEM((FFN_SLOTS, tf, D_MODEL), BF16),
            pltpu.SemaphoreType.DMA((3, FFN_SLOTS)),
        ],
        compiler_params=pltpu.CompilerParams(
            dimension_semantics=("arbitrary",), vmem_limit_bytes=V7X_VMEM_LIMIT_BYTES),
        name=f"ffn_l{layer}",
    )(x, x, norm_g, final_g, w_gate, w_up, w_down)


@jax.jit
def _forward(x, mix_norm_g, w_in, sink, conv_dw_w, conv_dw_b, conv_ln_g, conv_ln_b,
             sgu_ln_g, sgu_ln_b, sgu_w, sgu_b, w_out, ffn_norm_g, w_gate, w_up, w_down,
             final_norm_g):
    batch, seq, d = x.shape
    tokens = batch * seq
    ropes = _rope_tables(seq)
    w_in, w_out, w_gate, w_up, w_down, sgu_w = (
        w.astype(BF16) for w in (w_in, w_out, w_gate, w_up, w_down, sgu_w))
    vec3 = lambda p: p.reshape(DEPTH, 1, p.shape[-1])
    mix_norm_g, ffn_norm_g = vec3(mix_norm_g), vec3(ffn_norm_g)
    conv_dw_b, conv_ln_g, conv_ln_b = vec3(conv_dw_b), vec3(conv_ln_g), vec3(conv_ln_b)
    sgu_ln_g, sgu_ln_b = vec3(sgu_ln_g), vec3(sgu_ln_b)
    sgu_bt = jnp.swapaxes(sgu_b, 1, 2)
    wo_blocks = TB_MIX // CONV_ROWS
    w_out = w_out.reshape(DEPTH, d, wo_blocks, d // wo_blocks).transpose(0, 2, 1, 3)
    final_g = final_norm_g.reshape(1, d)

    xf = x.reshape(tokens, d)
    for layer in range(DEPTH):
        q, kv, c, gu = _in_proj(layer, xf, mix_norm_g, ropes, w_in, seq)
        to3 = lambda a: a.reshape(batch, seq, a.shape[-1])
        xn = _mixer(layer, sink, to3(xf), to3(q), to3(kv), to3(c), to3(gu), conv_dw_w, conv_dw_b,
                    conv_ln_g, conv_ln_b, sgu_ln_g, sgu_ln_b, sgu_w, sgu_bt, w_out)
        xf = _ffn(layer, xn.reshape(tokens, d), ffn_norm_g, w_gate, w_up, w_down, final_g,
                  apply_final_norm=(layer == DEPTH - 1))
    return xf.reshape(batch, seq, d)


def kernel(x, mix_norm_g, w_in, sink, conv_dw_w, conv_dw_b, conv_ln_g, conv_ln_b, sgu_ln_g, sgu_ln_b,
           sgu_w, sgu_b, w_out, ffn_norm_g, w_gate, w_up, w_down, final_norm_g):
    return _forward(x, mix_norm_g, w_in, sink, conv_dw_w, conv_dw_b, conv_ln_g, conv_ln_b,
                    sgu_ln_g, sgu_ln_b, sgu_w, sgu_b, w_out, ffn_norm_g, w_gate, w_up, w_down,
                    final_norm_g)
```

```python
import functools

import jax
import jax.numpy as jnp
import numpy as np
from jax import lax
from jax.experimental import pallas as pl
from jax.experimental.pallas import tpu as pltpu

F32 = jnp.float32
BF16 = jnp.bfloat16

D_MODEL = 2048
DEPTH = 4
HEAD_DIM = 128
ATTN_WIDTH = D_MODEL // 2
N_Q_HEADS = ATTN_WIDTH // HEAD_DIM
N_KV_HEADS = N_Q_HEADS // 4
Q_PER_KV = N_Q_HEADS // N_KV_HEADS
KV_WIDTH = N_KV_HEADS * HEAD_DIM
CONV_WIDTH = D_MODEL // 4
CONV_KERNEL = 31
CONV_PAD = (CONV_KERNEL - 1) // 2
SGU_WIDTH = D_MODEL // 4
SGU_HEADS = SGU_WIDTH // HEAD_DIM
CHUNK = 128
IN_WIDTH = ATTN_WIDTH + 2 * KV_WIDTH + 2 * CONV_WIDTH + 2 * SGU_WIDTH
WINDOW = 128
BLOCK = 128
ROPE_THETA = 500000.0
ROT_DIM = HEAD_DIM // 4
ROT_HALF = ROT_DIM // 2
D_FF = ((8 * D_MODEL // 3 + 255) // 256) * 256
EPS = 1e-6

NEXT_WEIGHT_SHAPES = ((D_MODEL, IN_WIDTH), (D_MODEL, D_MODEL), (D_MODEL, D_FF), (D_MODEL, D_FF),
                      (D_FF, D_MODEL))

K_START = ATTN_WIDTH
V_START = K_START + KV_WIDTH
CA_START = V_START + KV_WIDTH
CG_START = CA_START + CONV_WIDTH
UV_START = CG_START + CONV_WIDTH

V7X_VMEM_LIMIT_BYTES = 60 * 1024 * 1024
BF16_SUBLANE_TILE = 16
F32_SUBLANE_TILE = 8
TM_PROJ = 512
TB_MIX = 512
TM_FFN = 512
TF_FFN = 1024
FFN_SLOTS = 2
CONV_ROWS = 32
MASK_VALUE = -1e30
LOG2_E = np.float32(np.log2(np.e))
SCORE_SCALE_LOG2 = np.float32(np.log2(np.e) / np.sqrt(HEAD_DIM))


def _rms_norm_rows(x, g):
    ms = jnp.mean(x * x, axis=-1, keepdims=True)
    return x * lax.rsqrt(ms + EPS) * g


def _layer_norm_rows(x, g, b):
    mu = jnp.mean(x, axis=-1, keepdims=True)
    xc = x - mu
    var = jnp.mean(xc * xc, axis=-1, keepdims=True)
    return xc * lax.rsqrt(var + EPS) * g + b


def _rope_tables(seq):
    pos = jnp.arange(seq, dtype=F32)
    inv = ROPE_THETA ** (-jnp.arange(0, ROT_DIM, 2, dtype=F32) / ROT_DIM)
    ang = pos[:, None] * inv[None, :]
    cos, sin = jnp.cos(ang), jnp.sin(ang)
    rest = HEAD_DIM - ROT_DIM
    cos_t = jnp.concatenate([cos, cos, jnp.ones((seq, rest), F32)], axis=-1)
    sin_a = jnp.concatenate([-sin, jnp.zeros((seq, HEAD_DIM - ROT_HALF), F32)], axis=-1)
    sin_b = jnp.concatenate([jnp.zeros((seq, ROT_HALF), F32), sin, jnp.zeros((seq, rest), F32)], axis=-1)
    return cos_t, sin_a, sin_b


def _in_proj_kernel(x_ref, g_ref, cos_ref, sa_ref, sb_ref, w_ref, q_ref, kv_ref, c_ref, gu_ref, h_scr):
    h_scr[...] = _rms_norm_rows(x_ref[...], g_ref[...]).astype(BF16)
    cos, sin_a, sin_b = cos_ref[...], sa_ref[...], sb_ref[...]

    def rope(t):
        return (t * cos + pltpu.roll(t, HEAD_DIM - ROT_HALF, 1) * sin_a
                + pltpu.roll(t, ROT_HALF, 1) * sin_b)

    def proj(lo, hi):
        return jnp.dot(h_scr[...], w_ref[:, lo:hi], preferred_element_type=F32)

    zq = proj(0, K_START)
    for hd in range(N_Q_HEADS):
        sl = slice(hd * HEAD_DIM, (hd + 1) * HEAD_DIM)
        q_ref[:, sl] = (rope(zq[:, sl]) * SCORE_SCALE_LOG2).astype(BF16)
    zk = proj(K_START, V_START)
    for hd in range(N_KV_HEADS):
        sl = slice(hd * HEAD_DIM, (hd + 1) * HEAD_DIM)
        kv_ref[:, sl] = rope(zk[:, sl]).astype(BF16)
    kv_ref[:, KV_WIDTH:] = proj(V_START, CA_START).astype(BF16)
    za = proj(CA_START, CG_START)
    zg = proj(CG_START, UV_START)
    c_ref[...] = (za * jax.nn.sigmoid(zg)).astype(BF16)
    zu = proj(UV_START, IN_WIDTH)
    gelu = 0.5 * zu * (1.0 + lax.erf(zu * np.float32(np.sqrt(0.5))))
    gu_ref[...] = gelu.astype(BF16)


def _in_proj(layer, x, norm_g, ropes, w_in, seq):
    tokens = x.shape[0]
    tm = TM_PROJ
    seq_blocks = seq // tm
    row = lambda i: (i, 0)
    rope_spec = pl.BlockSpec((tm, HEAD_DIM), lambda i: (i % seq_blocks, 0))
    out_w = (ATTN_WIDTH, 2 * KV_WIDTH, CONV_WIDTH, 2 * SGU_WIDTH)
    return pl.pallas_call(
        _in_proj_kernel,
        grid=(tokens // tm,),
        in_specs=[
            pl.BlockSpec((tm, D_MODEL), row),
            pl.BlockSpec((None, 1, D_MODEL), lambda i: (layer, 0, 0)),
            rope_spec, rope_spec, rope_spec,
            pl.BlockSpec((D_MODEL, IN_WIDTH), lambda i: (0, 0), pipeline_mode=pl.Buffered(1)),
        ],
        out_specs=[pl.BlockSpec((tm, w), row) for w in out_w],
        out_shape=[jax.ShapeDtypeStruct((tokens, w), BF16) for w in out_w],
        scratch_shapes=[pltpu.VMEM((tm, D_MODEL), BF16)],
        compiler_params=pltpu.CompilerParams(
            dimension_semantics=("arbitrary",), vmem_limit_bytes=V7X_VMEM_LIMIT_BYTES),
        name=f"in_proj_l{layer}",
    )(x, norm_g, *ropes, w_in)


def _mixer_kernel(layer, sink_ref, x_ref, q_ref, kvc_ref, kvp_ref, kvn_ref, cc_ref, cp_ref, cn_ref,
                  gu_ref, dww_ref, dwb_ref, clg_ref, clb_ref, slg_ref, slb_ref, sw_ref, sbt_ref, wo_ref,
                  o_ref, mix_scr, kv_scr, c_scr, wb_scr, y_scr):
    tb = q_ref.shape[0]
    conv_col = ATTN_WIDTH
    sgu_col = ATTN_WIDTH + CONV_WIDTH

    i = pl.program_id(1)
    is_first = (i == 0).astype(jnp.int32)
    is_last = (i == pl.num_programs(1) - 1).astype(jnp.int32)

    kv_scr[0:BLOCK, :] = kvp_ref[...]
    kv_scr[BLOCK:BLOCK + tb, :] = kvc_ref[...]
    kv_scr[BLOCK + tb:, :] = kvn_ref[...]
    r = lax.broadcasted_iota(jnp.int32, (BLOCK, BLOCK), 0)
    c = lax.broadcasted_iota(jnp.int32, (BLOCK, BLOCK), 1)
    prev_bias = jnp.where(c >= r, 0.0, MASK_VALUE).astype(F32)
    next_bias = jnp.where(c <= r, 0.0, MASK_VALUE).astype(F32)
    prev_bias_first = jnp.minimum(prev_bias, MASK_VALUE * is_first.astype(F32))
    next_bias_last = jnp.minimum(next_bias, MASK_VALUE * is_last.astype(F32))
    n_sub = tb // BLOCK
    for j in range(n_sub):
        lo_bias = prev_bias_first if j == 0 else prev_bias
        hi_bias = next_bias_last if j == n_sub - 1 else next_bias
        rows = slice(j * BLOCK, (j + 1) * BLOCK)
        win = slice(j * BLOCK, (j + 3) * BLOCK)
        for g in range(N_KV_HEADS):
            heads = range(g * Q_PER_KV, (g + 1) * Q_PER_KV)
            qg = jnp.concatenate(
                [q_ref[rows, hd * HEAD_DIM:(hd + 1) * HEAD_DIM] for hd in heads], axis=0)
            kwin = kv_scr[win, g * HEAD_DIM:(g + 1) * HEAD_DIM]
            vwin = kv_scr[win, KV_WIDTH + g * HEAD_DIM:KV_WIDTH + (g + 1) * HEAD_DIM]
            s = lax.dot_general(qg, kwin, (((1,), (1,)), ((), ())), preferred_element_type=F32)
            probs, inv_l = [], []
            for hh, hd in enumerate(heads):
                sink = sink_ref[layer, hd] * LOG2_E
                hrows = slice(hh * BLOCK, (hh + 1) * BLOCK)
                sh = jnp.concatenate([s[hrows, 0:BLOCK] + lo_bias, s[hrows, BLOCK:2 * BLOCK],
                                      s[hrows, 2 * BLOCK:] + hi_bias], axis=1)
                m = jnp.maximum(jnp.max(sh, axis=-1, keepdims=True), sink)
                p = jnp.exp2(sh - m)
                denom = jnp.sum(p, axis=-1, keepdims=True) + jnp.exp2(sink - m)
                probs.append(p.astype(BF16))
                inv_l.append(1.0 / denom)
            o = jnp.dot(jnp.concatenate(probs, axis=0), vwin, preferred_element_type=F32)
            for hh, hd in enumerate(heads):
                oh = o[hh * BLOCK:(hh + 1) * BLOCK, :] * inv_l[hh]
                mix_scr[rows, hd * HEAD_DIM:(hd + 1) * HEAD_DIM] = oh.astype(BF16)

    for ci in range(tb // CHUNK):
        rows = slice(ci * CHUNK, (ci + 1) * CHUNK)
        v = _layer_norm_rows(gu_ref[rows, SGU_WIDTH:].astype(F32), slg_ref[...], slb_ref[...])
        v = v.astype(BF16)
        for hd in range(SGU_HEADS):
            cols = slice(hd * HEAD_DIM, (hd + 1) * HEAD_DIM)
            sp = jnp.dot(sw_ref[hd], v[:, cols], preferred_element_type=F32) + sbt_ref[:, hd:hd + 1]
            u = gu_ref[rows, cols].astype(F32)
            mix_scr[rows, sgu_col + hd * HEAD_DIM:sgu_col + (hd + 1) * HEAD_DIM] = (u * sp).astype(BF16)

    halo = cp_ref.shape[0]
    c_scr[0, 0:halo, :] = cp_ref[...].astype(F32) * (1 - is_first).astype(F32)
    c_scr[0, halo:halo + tb, :] = cc_ref[...].astype(F32)
    c_scr[0, halo + tb:, :] = cn_ref[...].astype(F32) * (1 - is_last).astype(F32)
    shifted_rows = tb + 2 * halo - F32_SUBLANE_TILE
    for s in range(1, F32_SUBLANE_TILE):
        c_scr[s, 0:shifted_rows, :] = c_scr[0, s:s + shifted_rows, :]
    first_tap = halo - CONV_PAD

    for k in range(CONV_KERNEL):
        wb_scr[k] = jnp.broadcast_to(dww_ref[k:k + 1, :], (F32_SUBLANE_TILE, CONV_WIDTH))
    wb_scr[CONV_KERNEL] = jnp.broadcast_to(dwb_ref[...], (F32_SUBLANE_TILE, CONV_WIDTH))
    groups = CONV_ROWS // F32_SUBLANE_TILE

    def conv_rows(t, carry):
        r0 = pl.multiple_of(t * CONV_ROWS, CONV_ROWS)
        acc = [wb_scr[CONV_KERNEL]] * groups
        for k in range(CONV_KERNEL):
            shift = (first_tap + k) % F32_SUBLANE_TILE
            w = wb_scr[k]
            for gi in range(groups):
                start = pl.multiple_of(
                    r0 + (first_tap + k - shift + gi * F32_SUBLANE_TILE), F32_SUBLANE_TILE)
                acc[gi] = acc[gi] + c_scr[shift, pl.ds(start, F32_SUBLANE_TILE), :] * w
        for gi in range(groups):
            out_row = pl.multiple_of(r0 + gi * F32_SUBLANE_TILE, F32_SUBLANE_TILE)
            y_scr[pl.ds(out_row, F32_SUBLANE_TILE), :] = acc[gi]
        return carry

    lax.fori_loop(0, tb // CONV_ROWS, conv_rows, 0)
    y = _layer_norm_rows(y_scr[...], clg_ref[...], clb_ref[...])
    mix_scr[:, conv_col:sgu_col] = (y * jax.nn.sigmoid(y)).astype(BF16)

    o_ref[...] = x_ref[...] + jnp.dot(mix_scr[...], wo_ref[...], preferred_element_type=F32)


def _mixer(layer, sink, x, q, kv, c, gu, dww, dwb, clg, clb, slg, slb, sgu_w, sgu_bt, w_out):
    batch, seq, _ = q.shape
    tb = TB_MIX
    halo = BF16_SUBLANE_TILE
    assert CONV_PAD <= halo and tb % BLOCK == 0 and tb % CONV_ROWS == 0
    kv_per = tb // BLOCK
    c_per = tb // halo
    cur = lambda b, i: (b, i, 0)
    kv_prev = lambda b, i: (b, jnp.maximum(i * kv_per - 1, 0), 0)
    kv_next = lambda b, i: (b, jnp.minimum((i + 1) * kv_per, seq // BLOCK - 1), 0)
    c_prev = lambda b, i: (b, jnp.maximum(i * c_per - 1, 0), 0)
    c_next = lambda b, i: (b, jnp.minimum((i + 1) * c_per, seq // halo - 1), 0)
    per_layer = lambda b, i: (layer, 0, 0)
    vec = lambda w: pl.BlockSpec((None, 1, w), per_layer)
    return pl.pallas_call(
        functools.partial(_mixer_kernel, layer),
        grid=(batch, seq // tb),
        in_specs=[
            pl.BlockSpec(memory_space=pltpu.SMEM),
            pl.BlockSpec((None, tb, D_MODEL), cur),
            pl.BlockSpec((None, tb, ATTN_WIDTH), cur),
            pl.BlockSpec((None, tb, 2 * KV_WIDTH), cur),
            pl.BlockSpec((None, BLOCK, 2 * KV_WIDTH), kv_prev),
            pl.BlockSpec((None, BLOCK, 2 * KV_WIDTH), kv_next),
            pl.BlockSpec((None, tb, CONV_WIDTH), cur),
            pl.BlockSpec((None, halo, CONV_WIDTH), c_prev),
            pl.BlockSpec((None, halo, CONV_WIDTH), c_next),
            pl.BlockSpec((None, tb, 2 * SGU_WIDTH), cur),
            pl.BlockSpec((None, CONV_KERNEL, CONV_WIDTH), per_layer),
            vec(CONV_WIDTH), vec(CONV_WIDTH), vec(CONV_WIDTH), vec(SGU_WIDTH), vec(SGU_WIDTH),
            pl.BlockSpec((None, SGU_HEADS, CHUNK, CHUNK), lambda b, i: (layer, 0, 0, 0)),
            pl.BlockSpec((None, CHUNK, SGU_HEADS), per_layer),
            pl.BlockSpec((D_MODEL, D_MODEL), lambda b, i: (0, 0), pipeline_mode=pl.Buffered(1)),
        ],
        out_specs=pl.BlockSpec((None, tb, D_MODEL), cur),
        out_shape=jax.ShapeDtypeStruct((batch, seq, D_MODEL), F32),
        scratch_shapes=[
            pltpu.VMEM((tb, D_MODEL), BF16),
            pltpu.VMEM((tb + 2 * BLOCK, 2 * KV_WIDTH), BF16),
            pltpu.VMEM((F32_SUBLANE_TILE, tb + 2 * halo, CONV_WIDTH), F32),
            pltpu.VMEM((CONV_KERNEL + 1, F32_SUBLANE_TILE, CONV_WIDTH), F32),
            pltpu.VMEM((tb, CONV_WIDTH), F32),
        ],
        compiler_params=pltpu.CompilerParams(
            dimension_semantics=("arbitrary", "arbitrary"), vmem_limit_bytes=V7X_VMEM_LIMIT_BYTES),
        name=f"mixer_l{layer}",
    )(sink, x, q, kv, kv, kv, c, c, c, gu, dww, dwb, clg, clb, slg, slb, sgu_w, sgu_bt, w_out)


def _ffn_kernel(next_layer, apply_final_norm, *refs):
    n_mats = len(NEXT_WEIGHT_SHAPES) if next_layer is not None else 0
    x_ref, g_ref, fg_ref, wg_hbm, wu_hbm, wd_hbm = refs[:6]
    nxt_f32 = refs[6:6 + n_mats]
    o_ref = refs[6 + n_mats]
    nxt_bf16 = refs[7 + n_mats:7 + 2 * n_mats]
    h_scr, wg_buf, wu_buf, wd_buf, sems = refs[7 + 2 * n_mats:12 + 2 * n_mats]
    cvt_in = refs[12 + 2 * n_mats:12 + 3 * n_mats]
    cvt_out = refs[12 + 3 * n_mats:12 + 4 * n_mats]
    cvt_sems = refs[12 + 4 * n_mats] if n_mats else None

    i = pl.program_id(0)
    n_tiles = pl.num_programs(0)
    chunks = _ffn_chunks()
    n_chunks = len(chunks)
    not_last_tile = i < n_tiles - 1

    def gate_up_copies(c):
        slot, (lo, width) = c % FFN_SLOTS, chunks[c]
        return (
            pltpu.make_async_copy(wg_hbm.at[:, pl.ds(lo, width)],
                                  wg_buf.at[slot, :, pl.ds(0, width)], sems.at[0, slot]),
            pltpu.make_async_copy(wu_hbm.at[:, pl.ds(lo, width)],
                                  wu_buf.at[slot, :, pl.ds(0, width)], sems.at[1, slot]),
        )

    def down_copies(c):
        slot, (lo, width) = c % FFN_SLOTS, chunks[c]
        return (
            pltpu.make_async_copy(wd_hbm.at[pl.ds(lo, width), :],
                                  wd_buf.at[slot, pl.ds(0, width), :], sems.at[2, slot]),
        )

    def slab_in(a, tile):
        rows = cvt_in[a].shape[0]
        return pltpu.make_async_copy(nxt_f32[a].at[next_layer, pl.ds(tile * rows, rows), :],
                                     cvt_in[a], cvt_sems.at[0, a])

    def slab_out(a, tile):
        rows = cvt_out[a].shape[0]
        return pltpu.make_async_copy(cvt_out[a], nxt_bf16[a].at[pl.ds(tile * rows, rows), :],
                                     cvt_sems.at[1, a])

    def start(copies_of, c):
        def issue():
            for copy in copies_of(c % n_chunks):
                copy.start()
        if c < n_chunks:
            issue()
        else:
            pl.when(not_last_tile)(issue)

    def wait(copies_of, c):
        for copy in copies_of(c):
            copy.wait()

    @pl.when(i == 0)
    def _():
        start(gate_up_copies, 0)
        start(gate_up_copies, 1)
        start(down_copies, 0)
        for a in range(n_mats):
            slab_in(a, 0).start()
            cvt_out[a][...] = jnp.zeros(cvt_out[a].shape, BF16)
            slab_out(a, 0).start()

    x = x_ref[...]
    o_ref[...] = x
    h_scr[...] = _rms_norm_rows(x, g_ref[...]).astype(BF16)
    wait(gate_up_copies, 0)

    for c, (_, width) in enumerate(chunks):
        slot = c % FFN_SLOTS
        h = h_scr[...]
        gate = jnp.dot(h, wg_buf[slot, :, 0:width], preferred_element_type=F32)
        up = jnp.dot(h, wu_buf[slot, :, 0:width], preferred_element_type=F32)
        act = (gate * jax.nn.sigmoid(gate) * up).astype(BF16)
        start(gate_up_copies, c + 2)
        start(down_copies, c + 1)
        if c + 1 < n_chunks:
            wait(gate_up_copies, c + 1)
        wait(down_copies, c)
        if c < n_mats:
            slab_in(c, i).wait()
            slab_out(c, jnp.maximum(i - 1, 0)).wait()
            cvt_out[c][...] = cvt_in[c][...].astype(BF16)
            slab_out(c, i).start()
            slab_in(c, jnp.minimum(i + 1, n_tiles - 1)).start()
        o_ref[...] += jnp.dot(act, wd_buf[slot, 0:width, :], preferred_element_type=F32)

    if apply_final_norm:
        o_ref[...] = _rms_norm_rows(o_ref[...], fg_ref[...])

    if n_mats:
        @pl.when(i == n_tiles - 1)
        def _():
            for a in range(n_mats):
                slab_in(a, i).wait()
                slab_out(a, i).wait()


def _ffn_chunks():
    bounds = list(range(0, D_FF, TF_FFN)) + [D_FF]
    return tuple((lo, hi - lo) for lo, hi in zip(bounds[:-1], bounds[1:]))


def _ffn(layer, x, norm_g, w_gate, w_up, w_down, final_g, apply_final_norm, next_f32=()):
    tokens = x.shape[0]
    tm, tf = TM_FFN, TF_FFN
    n_tiles = tokens // tm
    assert FFN_SLOTS == 2 and len(_ffn_chunks()) % FFN_SLOTS == 0
    assert len(next_f32) in (0, len(NEXT_WEIGHT_SHAPES)) and len(next_f32) <= len(_ffn_chunks())
    slabs = []
    for w, (rows, cols) in zip(next_f32, NEXT_WEIGHT_SHAPES):
        assert w.shape[1:] == (rows, cols) and rows % (n_tiles * BF16_SUBLANE_TILE) == 0
        slabs.append((rows // n_tiles, cols))
    row = lambda i: (i, 0)
    hbm = pl.BlockSpec(memory_space=pl.ANY)
    outs = pl.pallas_call(
        functools.partial(_ffn_kernel, layer + 1 if next_f32 else None, apply_final_norm),
        grid=(n_tiles,),
        in_specs=[
            pl.BlockSpec((tm, D_MODEL), row),
            pl.BlockSpec((None, 1, D_MODEL), lambda i: (layer, 0, 0)),
            pl.BlockSpec((1, D_MODEL), lambda i: (0, 0)),
            hbm, hbm, hbm,
        ] + [hbm] * len(next_f32),
        out_specs=[pl.BlockSpec((tm, D_MODEL), row)] + [hbm] * len(next_f32),
        out_shape=[jax.ShapeDtypeStruct((tokens, D_MODEL), F32)]
        + [jax.ShapeDtypeStruct(shape, BF16) for shape in NEXT_WEIGHT_SHAPES[:len(next_f32)]],
        scratch_shapes=[
            pltpu.VMEM((tm, D_MODEL), BF16),
            pltpu.VMEM((FFN_SLOTS, D_MODEL, tf), BF16),
            pltpu.VMEM((FFN_SLOTS, D_MODEL, tf), BF16),
            pltpu.VMEM((FFN_SLOTS, tf, D_MODEL), BF16),
            pltpu.SemaphoreType.DMA((3, FFN_SLOTS)),
        ] + [pltpu.VMEM(slab, F32) for slab in slabs] + [pltpu.VMEM(slab, BF16) for slab in slabs]
        + ([pltpu.SemaphoreType.DMA((2, len(slabs)))] if slabs else []),
        compiler_params=pltpu.CompilerParams(
            dimension_semantics=("arbitrary",), vmem_limit_bytes=V7X_VMEM_LIMIT_BYTES),
        name=f"ffn_l{layer}",
    )(x, norm_g, final_g, w_gate, w_up, w_down, *next_f32)
    return outs[0], tuple(outs[1:])


@jax.jit
def _forward(x, mix_norm_g, w_in, sink, conv_dw_w, conv_dw_b, conv_ln_g, conv_ln_b,
             sgu_ln_g, sgu_ln_b, sgu_w, sgu_b, w_out, ffn_norm_g, w_gate, w_up, w_down,
             final_norm_g):
    batch, seq, d = x.shape
    tokens = batch * seq
    ropes = _rope_tables(seq)
    weights_f32 = (w_in, w_out, w_gate, w_up, w_down)
    weights = tuple(w[0].astype(BF16) for w in weights_f32)
    sgu_w = sgu_w.astype(BF16)
    vec3 = lambda p: p.reshape(DEPTH, 1, p.shape[-1])
    mix_norm_g, ffn_norm_g = vec3(mix_norm_g), vec3(ffn_norm_g)
    conv_dw_b, conv_ln_g, conv_ln_b = vec3(conv_dw_b), vec3(conv_ln_g), vec3(conv_ln_b)
    sgu_ln_g, sgu_ln_b = vec3(sgu_ln_g), vec3(sgu_ln_b)
    sgu_bt = jnp.swapaxes(sgu_b, 1, 2)
    final_g = final_norm_g.reshape(1, d)

    xf = x.reshape(tokens, d)
    for layer in range(DEPTH):
        w_in_l, w_out_l, w_gate_l, w_up_l, w_down_l = weights
        last = layer == DEPTH - 1
        q, kv, c, gu = _in_proj(layer, xf, mix_norm_g, ropes, w_in_l, seq)
        to3 = lambda a: a.reshape(batch, seq, a.shape[-1])
        xn = _mixer(layer, sink, to3(xf), to3(q), to3(kv), to3(c), to3(gu), conv_dw_w, conv_dw_b,
                    conv_ln_g, conv_ln_b, sgu_ln_g, sgu_ln_b, sgu_w, sgu_bt, w_out_l)
        xf, weights = _ffn(layer, xn.reshape(tokens, d), ffn_norm_g, w_gate_l, w_up_l, w_down_l,
                           final_g, apply_final_norm=last, next_f32=() if last else weights_f32)
    return xf.reshape(batch, seq, d)


def kernel(x, mix_norm_g, w_in, sink, conv_dw_w, conv_dw_b, conv_ln_g, conv_ln_b, sgu_ln_g, sgu_ln_b,
           sgu_w, sgu_b, w_out, ffn_norm_g, w_gate, w_up, w_down, final_norm_g):
    return _forward(x, mix_norm_g, w_in, sink, conv_dw_w, conv_dw_b, conv_ln_g, conv_ln_b,
                    sgu_ln_g, sgu_ln_b, sgu_w, sgu_b, w_out, ffn_norm_g, w_gate, w_up, w_down,
                    final_norm_g)
```

```python
import functools

import jax
import jax.numpy as jnp
import numpy as np
from jax import lax
from jax.experimental import pallas as pl
from jax.experimental.pallas import tpu as pltpu

F32 = jnp.float32
BF16 = jnp.bfloat16

D_MODEL = 2048
DEPTH = 4
HEAD_DIM = 128
ATTN_WIDTH = D_MODEL // 2
N_Q_HEADS = ATTN_WIDTH // HEAD_DIM
N_KV_HEADS = N_Q_HEADS // 4
Q_PER_KV = N_Q_HEADS // N_KV_HEADS
KV_WIDTH = N_KV_HEADS * HEAD_DIM
CONV_WIDTH = D_MODEL // 4
CONV_KERNEL = 31
CONV_PAD = (CONV_KERNEL - 1) // 2
SGU_WIDTH = D_MODEL // 4
SGU_HEADS = SGU_WIDTH // HEAD_DIM
CHUNK = 128
IN_WIDTH = ATTN_WIDTH + 2 * KV_WIDTH + 2 * CONV_WIDTH + 2 * SGU_WIDTH
WINDOW = 128
BLOCK = 128
ROPE_THETA = 500000.0
ROT_DIM = HEAD_DIM // 4
ROT_HALF = ROT_DIM // 2
D_FF = ((8 * D_MODEL // 3 + 255) // 256) * 256
EPS = 1e-6

NEXT_WEIGHT_SHAPES = ((D_MODEL, IN_WIDTH), (D_MODEL, D_MODEL), (D_MODEL, D_FF), (D_MODEL, D_FF),
                      (D_FF, D_MODEL))

K_START = ATTN_WIDTH
V_START = K_START + KV_WIDTH
CA_START = V_START + KV_WIDTH
CG_START = CA_START + CONV_WIDTH
UV_START = CG_START + CONV_WIDTH

V7X_VMEM_LIMIT_BYTES = 60 * 1024 * 1024
BF16_SUBLANE_TILE = 16
F32_SUBLANE_TILE = 8
TM_PROJ = 512
TB_MIX = 512
TM_FFN = 512
TF_FFN = 1024
FFN_SLOTS = 2
FFN_ROW_SPLIT = 2
CONV_ROWS = 32
MASK_VALUE = -1e30
LOG2_E = np.float32(np.log2(np.e))
SCORE_SCALE_LOG2 = np.float32(np.log2(np.e) / np.sqrt(HEAD_DIM))


def _rms_norm_rows(x, g):
    ms = jnp.mean(x * x, axis=-1, keepdims=True)
    return x * lax.rsqrt(ms + EPS) * g


def _layer_norm_rows(x, g, b):
    mu = jnp.mean(x, axis=-1, keepdims=True)
    xc = x - mu
    var = jnp.mean(xc * xc, axis=-1, keepdims=True)
    return xc * lax.rsqrt(var + EPS) * g + b


def _rope_tables(seq):
    pos = jnp.arange(seq, dtype=F32)
    inv = ROPE_THETA ** (-jnp.arange(0, ROT_DIM, 2, dtype=F32) / ROT_DIM)
    ang = pos[:, None] * inv[None, :]
    cos, sin = jnp.cos(ang), jnp.sin(ang)
    rest = HEAD_DIM - ROT_DIM
    cos_t = jnp.concatenate([cos, cos, jnp.ones((seq, rest), F32)], axis=-1)
    sin_a = jnp.concatenate([-sin, jnp.zeros((seq, HEAD_DIM - ROT_HALF), F32)], axis=-1)
    sin_b = jnp.concatenate([jnp.zeros((seq, ROT_HALF), F32), sin, jnp.zeros((seq, rest), F32)], axis=-1)
    return cos_t, sin_a, sin_b


class _SlabCaster:
    def __init__(self, layer, src_refs, dst_refs, in_bufs, out_bufs, sems):
        self.layer, self.src_refs, self.dst_refs = layer, src_refs, dst_refs
        self.in_bufs, self.out_bufs, self.sems = in_bufs, out_bufs, sems

    def __len__(self):
        return len(self.src_refs)

    def _read(self, a, step):
        rows = self.in_bufs[a].shape[0]
        return pltpu.make_async_copy(self.src_refs[a].at[self.layer, pl.ds(step * rows, rows), :],
                                     self.in_bufs[a], self.sems.at[0, a])

    def _write(self, a, step):
        rows = self.out_bufs[a].shape[0]
        return pltpu.make_async_copy(self.out_bufs[a], self.dst_refs[a].at[pl.ds(step * rows, rows), :],
                                     self.sems.at[1, a])

    def prime(self):
        for a in range(len(self)):
            self._read(a, 0).start()
            self.out_bufs[a][...] = jnp.zeros(self.out_bufs[a].shape, BF16)
            self._write(a, 0).start()

    def cast(self, a, step, n_steps):
        self._read(a, step).wait()
        self._write(a, jnp.maximum(step - 1, 0)).wait()
        self.out_bufs[a][...] = self.in_bufs[a][...].astype(BF16)
        self._write(a, step).start()
        self._read(a, jnp.minimum(step + 1, n_steps - 1)).start()

    def drain(self, step):
        for a in range(len(self)):
            self._read(a, step).wait()
            self._write(a, step).wait()


def _slab_cast_specs(mats_f32, shapes, n_steps):
    slabs = []
    for w, (rows, cols) in zip(mats_f32, shapes):
        assert w.shape[1:] == (rows, cols) and rows % (n_steps * BF16_SUBLANE_TILE) == 0
        slabs.append((rows // n_steps, cols))
    out_shapes = [jax.ShapeDtypeStruct(shape, BF16) for shape in shapes[:len(mats_f32)]]
    scratch = [pltpu.VMEM(slab, F32) for slab in slabs] + [pltpu.VMEM(slab, BF16) for slab in slabs]
    if slabs:
        scratch.append(pltpu.SemaphoreType.DMA((2, len(slabs))))
    return out_shapes, scratch


def _in_proj_kernel(cast_layer, n_cast, *refs):
    x_ref, g_ref, cos_ref, sa_ref, sb_ref, w_ref = refs[:6]
    src_refs = refs[6:6 + n_cast]
    q_ref, kv_ref, c_ref, gu_ref = refs[6 + n_cast:10 + n_cast]
    dst_refs = refs[10 + n_cast:10 + 2 * n_cast]
    h_scr = refs[10 + 2 * n_cast]
    in_bufs = refs[11 + 2 * n_cast:11 + 3 * n_cast]
    out_bufs = refs[11 + 3 * n_cast:11 + 4 * n_cast]
    caster = _SlabCaster(cast_layer, src_refs, dst_refs, in_bufs, out_bufs,
                         refs[11 + 4 * n_cast] if n_cast else None)
    step, n_steps = pl.program_id(0), pl.num_programs(0)
    if n_cast:
        pl.when(step == 0)(caster.prime)
    tm = x_ref.shape[0]

    def rope(t, rows):
        return (t * cos_ref[rows, :] + pltpu.roll(t, HEAD_DIM - ROT_HALF, 1) * sa_ref[rows, :]
                + pltpu.roll(t, ROT_HALF, 1) * sb_ref[rows, :])

    def proj(lo, hi, rows=slice(None)):
        return jnp.dot(h_scr[rows, :], w_ref[:, lo:hi], preferred_element_type=F32)

    for half in range(2):
        rows = slice(half * tm // 2, (half + 1) * tm // 2)
        h_scr[rows, :] = _rms_norm_rows(x_ref[rows, :], g_ref[...]).astype(BF16)
        zq = proj(0, K_START, rows)
        for hd in range(N_Q_HEADS):
            sl = slice(hd * HEAD_DIM, (hd + 1) * HEAD_DIM)
            q_ref[rows, sl] = (rope(zq[:, sl], rows) * SCORE_SCALE_LOG2).astype(BF16)
    zk = proj(K_START, V_START)
    for hd in range(N_KV_HEADS):
        sl = slice(hd * HEAD_DIM, (hd + 1) * HEAD_DIM)
        kv_ref[:, sl] = rope(zk[:, sl], slice(None)).astype(BF16)
    kv_ref[:, KV_WIDTH:] = proj(V_START, CA_START).astype(BF16)
    za = proj(CA_START, CG_START)
    zg = proj(CG_START, UV_START)
    c_ref[...] = (za * jax.nn.sigmoid(zg)).astype(BF16)
    zu = proj(UV_START, IN_WIDTH)
    gelu = 0.5 * zu * (1.0 + lax.erf(zu * np.float32(np.sqrt(0.5))))
    gu_ref[...] = gelu.astype(BF16)

    for a in range(n_cast):
        caster.cast(a, step, n_steps)
    if n_cast:
        pl.when(step == n_steps - 1)(lambda: caster.drain(step))


def _in_proj(layer, x, norm_g, ropes, w_in, seq, cast_f32=()):
    tokens = x.shape[0]
    tm = TM_PROJ
    n_steps = tokens // tm
    seq_blocks = seq // tm
    row = lambda i: (i, 0)
    hbm = pl.BlockSpec(memory_space=pl.ANY)
    rope_spec = pl.BlockSpec((tm, HEAD_DIM), lambda i: (i % seq_blocks, 0))
    out_w = (ATTN_WIDTH, 2 * KV_WIDTH, CONV_WIDTH, 2 * SGU_WIDTH)
    cast_shapes, cast_scratch = _slab_cast_specs(cast_f32, NEXT_WEIGHT_SHAPES[1:], n_steps)
    outs = pl.pallas_call(
        functools.partial(_in_proj_kernel, layer, len(cast_f32)),
        grid=(n_steps,),
        in_specs=[
            pl.BlockSpec((tm, D_MODEL), row),
            pl.BlockSpec((None, 1, D_MODEL), lambda i: (layer, 0, 0)),
            rope_spec, rope_spec, rope_spec,
            pl.BlockSpec((D_MODEL, IN_WIDTH), lambda i: (0, 0), pipeline_mode=pl.Buffered(1)),
        ] + [hbm] * len(cast_f32),
        out_specs=[pl.BlockSpec((tm, w), row) for w in out_w] + [hbm] * len(cast_f32),
        out_shape=[jax.ShapeDtypeStruct((tokens, w), BF16) for w in out_w] + cast_shapes,
        scratch_shapes=[pltpu.VMEM((tm, D_MODEL), BF16)] + cast_scratch,
        compiler_params=pltpu.CompilerParams(
            dimension_semantics=("arbitrary",), vmem_limit_bytes=V7X_VMEM_LIMIT_BYTES),
        name=f"in_proj_l{layer}",
    )(x, norm_g, *ropes, w_in, *cast_f32)
    return tuple(outs[:len(out_w)]), tuple(outs[len(out_w):])


def _mixer_kernel(layer, sink_ref, x_ref, q_ref, kvc_ref, kvp_ref, kvn_ref, cc_ref, cp_ref, cn_ref,
                  gu_ref, dww_ref, dwb_ref, clg_ref, clb_ref, slg_ref, slb_ref, sw_ref, sbt_ref, wo_ref,
                  o_ref, mix_scr, kv_scr, c_scr, wb_scr, y_scr):
    tb = q_ref.shape[0]
    conv_col = ATTN_WIDTH
    sgu_col = ATTN_WIDTH + CONV_WIDTH

    i = pl.program_id(1)
    is_first = (i == 0).astype(jnp.int32)
    is_last = (i == pl.num_programs(1) - 1).astype(jnp.int32)

    kv_scr[0:BLOCK, :] = kvp_ref[...]
    kv_scr[BLOCK:BLOCK + tb, :] = kvc_ref[...]
    kv_scr[BLOCK + tb:, :] = kvn_ref[...]
    r = lax.broadcasted_iota(jnp.int32, (BLOCK, BLOCK), 0)
    c = lax.broadcasted_iota(jnp.int32, (BLOCK, BLOCK), 1)
    prev_bias = jnp.where(c >= r, 0.0, MASK_VALUE).astype(F32)
    next_bias = jnp.where(c <= r, 0.0, MASK_VALUE).astype(F32)
    prev_bias_first = jnp.minimum(prev_bias, MASK_VALUE * is_first.astype(F32))
    next_bias_last = jnp.minimum(next_bias, MASK_VALUE * is_last.astype(F32))
    n_sub = tb // BLOCK
    for j in range(n_sub):
        lo_bias = prev_bias_first if j == 0 else prev_bias
        hi_bias = next_bias_last if j == n_sub - 1 else next_bias
        rows = slice(j * BLOCK, (j + 1) * BLOCK)
        win = slice(j * BLOCK, (j + 3) * BLOCK)
        for g in range(N_KV_HEADS):
            heads = range(g * Q_PER_KV, (g + 1) * Q_PER_KV)
            qg = jnp.concatenate(
                [q_ref[rows, hd * HEAD_DIM:(hd + 1) * HEAD_DIM] for hd in heads], axis=0)
            kwin = kv_scr[win, g * HEAD_DIM:(g + 1) * HEAD_DIM]
            vwin = kv_scr[win, KV_WIDTH + g * HEAD_DIM:KV_WIDTH + (g + 1) * HEAD_DIM]
            s = lax.dot_general(qg, kwin, (((1,), (1,)), ((), ())), preferred_element_type=F32)
            probs, inv_l = [], []
            for hh, hd in enumerate(heads):
                sink = sink_ref[layer, hd] * LOG2_E
                hrows = slice(hh * BLOCK, (hh + 1) * BLOCK)
                sh = jnp.concatenate([s[hrows, 0:BLOCK] + lo_bias, s[hrows, BLOCK:2 * BLOCK],
                                      s[hrows, 2 * BLOCK:] + hi_bias], axis=1)
                m = jnp.maximum(jnp.max(sh, axis=-1, keepdims=True), sink)
                p = jnp.exp2(sh - m)
                denom = jnp.sum(p, axis=-1, keepdims=True) + jnp.exp2(sink - m)
                probs.append(p.astype(BF16))
                inv_l.append(1.0 / denom)
            o = jnp.dot(jnp.concatenate(probs, axis=0), vwin, preferred_element_type=F32)
            for hh, hd in enumerate(heads):
                oh = o[hh * BLOCK:(hh + 1) * BLOCK, :] * inv_l[hh]
                mix_scr[rows, hd * HEAD_DIM:(hd + 1) * HEAD_DIM] = oh.astype(BF16)

    for ci in range(tb // CHUNK):
        rows = slice(ci * CHUNK, (ci + 1) * CHUNK)
        v = _layer_norm_rows(gu_ref[rows, SGU_WIDTH:].astype(F32), slg_ref[...], slb_ref[...])
        v = v.astype(BF16)
        for hd in range(SGU_HEADS):
            cols = slice(hd * HEAD_DIM, (hd + 1) * HEAD_DIM)
            sp = jnp.dot(sw_ref[hd], v[:, cols], preferred_element_type=F32) + sbt_ref[:, hd:hd + 1]
            u = gu_ref[rows, cols].astype(F32)
            mix_scr[rows, sgu_col + hd * HEAD_DIM:sgu_col + (hd + 1) * HEAD_DIM] = (u * sp).astype(BF16)

    halo = cp_ref.shape[0]
    c_scr[0, 0:halo, :] = cp_ref[...].astype(F32) * (1 - is_first).astype(F32)
    c_scr[0, halo:halo + tb, :] = cc_ref[...].astype(F32)
    c_scr[0, halo + tb:, :] = cn_ref[...].astype(F32) * (1 - is_last).astype(F32)
    shifted_rows = tb + 2 * halo - F32_SUBLANE_TILE
    for s in range(1, F32_SUBLANE_TILE):
        c_scr[s, 0:shifted_rows, :] = c_scr[0, s:s + shifted_rows, :]
    first_tap = halo - CONV_PAD

    for k in range(CONV_KERNEL):
        wb_scr[k] = jnp.broadcast_to(dww_ref[k:k + 1, :], (F32_SUBLANE_TILE, CONV_WIDTH))
    wb_scr[CONV_KERNEL] = jnp.broadcast_to(dwb_ref[...], (F32_SUBLANE_TILE, CONV_WIDTH))
    groups = CONV_ROWS // F32_SUBLANE_TILE

    def conv_rows(t, carry):
        r0 = pl.multiple_of(t * CONV_ROWS, CONV_ROWS)
        acc = [wb_scr[CONV_KERNEL]] * groups
        for k in range(CONV_KERNEL):
            shift = (first_tap + k) % F32_SUBLANE_TILE
            w = wb_scr[k]
            for gi in range(groups):
                start = pl.multiple_of(
                    r0 + (first_tap + k - shift + gi * F32_SUBLANE_TILE), F32_SUBLANE_TILE)
                acc[gi] = acc[gi] + c_scr[shift, pl.ds(start, F32_SUBLANE_TILE), :] * w
        for gi in range(groups):
            out_row = pl.multiple_of(r0 + gi * F32_SUBLANE_TILE, F32_SUBLANE_TILE)
            y_scr[pl.ds(out_row, F32_SUBLANE_TILE), :] = acc[gi]
        return carry

    lax.fori_loop(0, tb // CONV_ROWS, conv_rows, 0)
    y = _layer_norm_rows(y_scr[...], clg_ref[...], clb_ref[...])
    mix_scr[:, conv_col:sgu_col] = (y * jax.nn.sigmoid(y)).astype(BF16)

    o_ref[...] = x_ref[...] + jnp.dot(mix_scr[...], wo_ref[...], preferred_element_type=F32)


def _mixer(layer, sink, x, q, kv, c, gu, dww, dwb, clg, clb, slg, slb, sgu_w, sgu_bt, w_out):
    batch, seq, _ = q.shape
    tb = TB_MIX
    halo = BF16_SUBLANE_TILE
    assert CONV_PAD <= halo and tb % BLOCK == 0 and tb % CONV_ROWS == 0
    kv_per = tb // BLOCK
    c_per = tb // halo
    cur = lambda b, i: (b, i, 0)
    kv_prev = lambda b, i: (b, jnp.maximum(i * kv_per - 1, 0), 0)
    kv_next = lambda b, i: (b, jnp.minimum((i + 1) * kv_per, seq // BLOCK - 1), 0)
    c_prev = lambda b, i: (b, jnp.maximum(i * c_per - 1, 0), 0)
    c_next = lambda b, i: (b, jnp.minimum((i + 1) * c_per, seq // halo - 1), 0)
    per_layer = lambda b, i: (layer, 0, 0)
    vec = lambda w: pl.BlockSpec((None, 1, w), per_layer)
    return pl.pallas_call(
        functools.partial(_mixer_kernel, layer),
        grid=(batch, seq // tb),
        in_specs=[
            pl.BlockSpec(memory_space=pltpu.SMEM),
            pl.BlockSpec((None, tb, D_MODEL), cur),
            pl.BlockSpec((None, tb, ATTN_WIDTH), cur),
            pl.BlockSpec((None, tb, 2 * KV_WIDTH), cur),
            pl.BlockSpec((None, BLOCK, 2 * KV_WIDTH), kv_prev),
            pl.BlockSpec((None, BLOCK, 2 * KV_WIDTH), kv_next),
            pl.BlockSpec((None, tb, CONV_WIDTH), cur),
            pl.BlockSpec((None, halo, CONV_WIDTH), c_prev),
            pl.BlockSpec((None, halo, CONV_WIDTH), c_next),
            pl.BlockSpec((None, tb, 2 * SGU_WIDTH), cur),
            pl.BlockSpec((None, CONV_KERNEL, CONV_WIDTH), per_layer),
            vec(CONV_WIDTH), vec(CONV_WIDTH), vec(CONV_WIDTH), vec(SGU_WIDTH), vec(SGU_WIDTH),
            pl.BlockSpec((None, SGU_HEADS, CHUNK, CHUNK), lambda b, i: (layer, 0, 0, 0)),
            pl.BlockSpec((None, CHUNK, SGU_HEADS), per_layer),
            pl.BlockSpec((D_MODEL, D_MODEL), lambda b, i: (0, 0), pipeline_mode=pl.Buffered(1)),
        ],
        out_specs=pl.BlockSpec((None, tb, D_MODEL), cur),
        out_shape=jax.ShapeDtypeStruct((batch, seq, D_MODEL), F32),
        scratch_shapes=[
            pltpu.VMEM((tb, D_MODEL), BF16),
            pltpu.VMEM((tb + 2 * BLOCK, 2 * KV_WIDTH), BF16),
            pltpu.VMEM((F32_SUBLANE_TILE, tb + 2 * halo, CONV_WIDTH), F32),
            pltpu.VMEM((CONV_KERNEL + 1, F32_SUBLANE_TILE, CONV_WIDTH), F32),
            pltpu.VMEM((tb, CONV_WIDTH), F32),
        ],
        compiler_params=pltpu.CompilerParams(
            dimension_semantics=("arbitrary", "arbitrary"), vmem_limit_bytes=V7X_VMEM_LIMIT_BYTES),
        name=f"mixer_l{layer}",
    )(sink, x, q, kv, kv, kv, c, c, c, gu, dww, dwb, clg, clb, slg, slb, sgu_w, sgu_bt, w_out)


def _ffn_kernel(next_layer, apply_final_norm, *refs):
    n_mats = len(NEXT_WEIGHT_SHAPES) if next_layer is not None else 0
    x_ref, g_ref, fg_ref, wg_hbm, wu_hbm, wd_hbm = refs[:6]
    o_ref = refs[6 + n_mats]
    h_scr, wgu_buf, wd_buf, sems = refs[7 + 2 * n_mats:11 + 2 * n_mats]
    caster = _SlabCaster(
        next_layer, refs[6:6 + n_mats], refs[7 + n_mats:7 + 2 * n_mats],
        refs[11 + 2 * n_mats:11 + 3 * n_mats], refs[11 + 3 * n_mats:11 + 4 * n_mats],
        refs[11 + 4 * n_mats] if n_mats else None)

    i = pl.program_id(0)
    n_tiles = pl.num_programs(0)
    chunks = _ffn_chunks()
    n_chunks = len(chunks)

    def gate_up_copies(c):
        slot, (lo, width) = c % FFN_SLOTS, chunks[c]
        return (
            pltpu.make_async_copy(wg_hbm.at[:, pl.ds(lo, width)],
                                  wgu_buf.at[slot, :, pl.ds(0, width)], sems.at[0, slot]),
            pltpu.make_async_copy(wu_hbm.at[:, pl.ds(lo, width)],
                                  wgu_buf.at[slot, :, pl.ds(width, width)], sems.at[1, slot]),
        )

    def down_copies(c):
        slot, (lo, width) = c % FFN_SLOTS, chunks[c]
        return (
            pltpu.make_async_copy(wd_hbm.at[pl.ds(lo, width), :],
                                  wd_buf.at[slot, pl.ds(0, width), :], sems.at[2, slot]),
        )

    def start(copies_of, c):
        for copy in copies_of(c % n_chunks):
            copy.start()

    def wait(copies_of, c):
        for copy in copies_of(c):
            copy.wait()

    @pl.when(i == 0)
    def _():
        start(gate_up_copies, 0)
        start(gate_up_copies, 1)
        start(down_copies, 0)
        caster.prime()

    tm = x_ref.shape[0]
    wait(gate_up_copies, 0)

    def gate_up_act(rows, slot, width):
        gate_up = jnp.dot(h_scr[rows, :], wgu_buf[slot, :, 0:2 * width], preferred_element_type=F32)
        gate, up = gate_up[:, 0:width], gate_up[:, width:]
        return (gate * jax.nn.sigmoid(gate) * up).astype(BF16)

    halves = [slice(half * tm // FFN_ROW_SPLIT, (half + 1) * tm // FFN_ROW_SPLIT)
              for half in range(FFN_ROW_SPLIT)]
    for c, (_, width) in enumerate(chunks):
        slot = c % FFN_SLOTS
        acts = []
        for rows in halves:
            if c == 0:
                x = x_ref[rows, :]
                o_ref[rows, :] = x
                h_scr[rows, :] = _rms_norm_rows(x, g_ref[...]).astype(BF16)
            acts.append(gate_up_act(rows, slot, width))
        start(gate_up_copies, c + 2)
        start(down_copies, c + 1)
        if c + 1 < n_chunks:
            wait(gate_up_copies, c + 1)
        wait(down_copies, c)
        if c < n_mats:
            caster.cast(c, i, n_tiles)
        for rows, act in zip(halves, acts):
            o_ref[rows, :] += jnp.dot(act, wd_buf[slot, 0:width, :], preferred_element_type=F32)

    if apply_final_norm:
        o_ref[...] = _rms_norm_rows(o_ref[...], fg_ref[...])

    @pl.when(i == n_tiles - 1)
    def _():
        wait(gate_up_copies, 0)
        wait(gate_up_copies, 1)
        wait(down_copies, 0)
        caster.drain(i)


def _ffn_chunks():
    bounds = list(range(0, D_FF, TF_FFN)) + [D_FF]
    return tuple((lo, hi - lo) for lo, hi in zip(bounds[:-1], bounds[1:]))


def _ffn(layer, x, norm_g, w_gate, w_up, w_down, final_g, apply_final_norm, next_f32=()):
    tokens = x.shape[0]
    tm, tf = TM_FFN, TF_FFN
    n_tiles = tokens // tm
    assert FFN_SLOTS == 2 and len(_ffn_chunks()) % FFN_SLOTS == 0
    assert len(next_f32) in (0, len(NEXT_WEIGHT_SHAPES)) and len(next_f32) <= len(_ffn_chunks())
    cast_shapes, cast_scratch = _slab_cast_specs(next_f32, NEXT_WEIGHT_SHAPES, n_tiles)
    row = lambda i: (i, 0)
    hbm = pl.BlockSpec(memory_space=pl.ANY)
    outs = pl.pallas_call(
        functools.partial(_ffn_kernel, layer + 1 if next_f32 else None, apply_final_norm),
        grid=(n_tiles,),
        in_specs=[
            pl.BlockSpec((tm, D_MODEL), row),
            pl.BlockSpec((None, 1, D_MODEL), lambda i: (layer, 0, 0)),
            pl.BlockSpec((1, D_MODEL), lambda i: (0, 0)),
            hbm, hbm, hbm,
        ] + [hbm] * len(next_f32),
        out_specs=[pl.BlockSpec((tm, D_MODEL), row)] + [hbm] * len(next_f32),
        out_shape=[jax.ShapeDtypeStruct((tokens, D_MODEL), F32)] + cast_shapes,
        scratch_shapes=[
            pltpu.VMEM((tm, D_MODEL), BF16),
            pltpu.VMEM((FFN_SLOTS, D_MODEL, 2 * tf), BF16),
            pltpu.VMEM((FFN_SLOTS, tf, D_MODEL), BF16),
            pltpu.SemaphoreType.DMA((3, FFN_SLOTS)),
        ] + cast_scratch,
        compiler_params=pltpu.CompilerParams(
            dimension_semantics=("arbitrary",), vmem_limit_bytes=V7X_VMEM_LIMIT_BYTES),
        name=f"ffn_l{layer}",
    )(x, norm_g, final_g, w_gate, w_up, w_down, *next_f32)
    return outs[0], tuple(outs[1:])


@jax.jit
def _forward(x, mix_norm_g, w_in, sink, conv_dw_w, conv_dw_b, conv_ln_g, conv_ln_b,
             sgu_ln_g, sgu_ln_b, sgu_w, sgu_b, w_out, ffn_norm_g, w_gate, w_up, w_down,
             final_norm_g):
    batch, seq, d = x.shape
    tokens = batch * seq
    ropes = _rope_tables(seq)
    weights_f32 = (w_in, w_out, w_gate, w_up, w_down)
    weights = (w_in[0].astype(BF16),)
    sgu_w = sgu_w.astype(BF16)
    vec3 = lambda p: p.reshape(DEPTH, 1, p.shape[-1])
    mix_norm_g, ffn_norm_g = vec3(mix_norm_g), vec3(ffn_norm_g)
    conv_dw_b, conv_ln_g, conv_ln_b = vec3(conv_dw_b), vec3(conv_ln_g), vec3(conv_ln_b)
    sgu_ln_g, sgu_ln_b = vec3(sgu_ln_g), vec3(sgu_ln_b)
    sgu_bt = jnp.swapaxes(sgu_b, 1, 2)
    final_g = final_norm_g.reshape(1, d)

    xf = x.reshape(tokens, d)
    for layer in range(DEPTH):
        last = layer == DEPTH - 1
        (q, kv, c, gu), cast_now = _in_proj(layer, xf, mix_norm_g, ropes, weights[0], seq,
                                            cast_f32=weights_f32[1:] if layer == 0 else ())
        _, w_out_l, w_gate_l, w_up_l, w_down_l = weights + cast_now
        to3 = lambda a: a.reshape(batch, seq, a.shape[-1])
        xn = _mixer(layer, sink, to3(xf), to3(q), to3(kv), to3(c), to3(gu), conv_dw_w, conv_dw_b,
                    conv_ln_g, conv_ln_b, sgu_ln_g, sgu_ln_b, sgu_w, sgu_bt, w_out_l)
        xf, weights = _ffn(layer, xn.reshape(tokens, d), ffn_norm_g, w_gate_l, w_up_l, w_down_l,
                           final_g, apply_final_norm=last, next_f32=() if last else weights_f32)
    return xf.reshape(batch, seq, d)


def kernel(x, mix_norm_g, w_in, sink, conv_dw_w, conv_dw_b, conv_ln_g, conv_ln_b, sgu_ln_g, sgu_ln_b,
           sgu_w, sgu_b, w_out, ffn_norm_g, w_gate, w_up, w_down, final_norm_g):
    return _forward(x, mix_norm_g, w_in, sink, conv_dw_w, conv_dw_b, conv_ln_g, conv_ln_b,
                    sgu_ln_g, sgu_ln_b, sgu_w, sgu_b, w_out, ffn_norm_g, w_gate, w_up, w_down,
                    final_norm_g)
```

```python
import functools

import jax
import jax.numpy as jnp
import numpy as np
from jax import lax
from jax.experimental import pallas as pl
from jax.experimental.pallas import tpu as pltpu

F32 = jnp.float32
BF16 = jnp.bfloat16

D_MODEL = 2048
DEPTH = 4
HEAD_DIM = 128
ATTN_WIDTH = D_MODEL // 2
N_Q_HEADS = ATTN_WIDTH // HEAD_DIM
N_KV_HEADS = N_Q_HEADS // 4
Q_PER_KV = N_Q_HEADS // N_KV_HEADS
KV_WIDTH = N_KV_HEADS * HEAD_DIM
CONV_WIDTH = D_MODEL // 4
CONV_KERNEL = 31
CONV_PAD = (CONV_KERNEL - 1) // 2
SGU_WIDTH = D_MODEL // 4
SGU_HEADS = SGU_WIDTH // HEAD_DIM
CHUNK = 128
IN_WIDTH = ATTN_WIDTH + 2 * KV_WIDTH + 2 * CONV_WIDTH + 2 * SGU_WIDTH
WINDOW = 128
BLOCK = 128
ROPE_THETA = 500000.0
ROT_DIM = HEAD_DIM // 4
ROT_HALF = ROT_DIM // 2
D_FF = ((8 * D_MODEL // 3 + 255) // 256) * 256
EPS = 1e-6

NEXT_WEIGHT_SHAPES = ((D_MODEL, IN_WIDTH), (D_MODEL, D_MODEL), (D_MODEL, D_FF), (D_MODEL, D_FF),
                      (D_FF, D_MODEL))

K_START = ATTN_WIDTH
V_START = K_START + KV_WIDTH
CA_START = V_START + KV_WIDTH
CG_START = CA_START + CONV_WIDTH
UV_START = CG_START + CONV_WIDTH

V7X_VMEM_LIMIT_BYTES = 60 * 1024 * 1024
BF16_SUBLANE_TILE = 16
F32_SUBLANE_TILE = 8
TM_PROJ = 512
TB_MIX = 512
TM_FFN = 512
TF_FFN = 1024
FFN_SLOTS = 2
FFN_ROW_SPLIT = 2
CONV_ROWS = 32
MASK_VALUE = -1e30
LOG2_E = np.float32(np.log2(np.e))
SCORE_SCALE_LOG2 = np.float32(np.log2(np.e) / np.sqrt(HEAD_DIM))


def _rms_norm_rows(x, g):
    ms = jnp.mean(x * x, axis=-1, keepdims=True)
    return x * lax.rsqrt(ms + EPS) * g


def _layer_norm_rows(x, g, b):
    mu = jnp.mean(x, axis=-1, keepdims=True)
    xc = x - mu
    var = jnp.mean(xc * xc, axis=-1, keepdims=True)
    return xc * lax.rsqrt(var + EPS) * g + b


def _rope_tables(seq):
    pos = jnp.arange(seq, dtype=F32)
    inv = ROPE_THETA ** (-jnp.arange(0, ROT_DIM, 2, dtype=F32) / ROT_DIM)
    ang = pos[:, None] * inv[None, :]
    cos, sin = jnp.cos(ang), jnp.sin(ang)
    rest = HEAD_DIM - ROT_DIM
    cos_t = jnp.concatenate([cos, cos, jnp.ones((seq, rest), F32)], axis=-1)
    sin_a = jnp.concatenate([-sin, jnp.zeros((seq, HEAD_DIM - ROT_HALF), F32)], axis=-1)
    sin_b = jnp.concatenate([jnp.zeros((seq, ROT_HALF), F32), sin, jnp.zeros((seq, rest), F32)], axis=-1)
    return cos_t, sin_a, sin_b


class _SlabCaster:
    def __init__(self, layer, src_refs, dst_refs, in_bufs, out_bufs, sems):
        self.layer, self.src_refs, self.dst_refs = layer, src_refs, dst_refs
        self.in_bufs, self.out_bufs, self.sems = in_bufs, out_bufs, sems

    def __len__(self):
        return len(self.src_refs)

    def _read(self, a, step):
        rows = self.in_bufs[a].shape[0]
        return pltpu.make_async_copy(self.src_refs[a].at[self.layer, pl.ds(step * rows, rows), :],
                                     self.in_bufs[a], self.sems.at[0, a])

    def _write(self, a, step):
        rows = self.out_bufs[a].shape[0]
        return pltpu.make_async_copy(self.out_bufs[a], self.dst_refs[a].at[pl.ds(step * rows, rows), :],
                                     self.sems.at[1, a])

    def prime(self):
        for a in range(len(self)):
            self._read(a, 0).start()
            self.out_bufs[a][...] = jnp.zeros(self.out_bufs[a].shape, BF16)
            self._write(a, 0).start()

    def cast(self, a, step, n_steps):
        self._read(a, step).wait()
        self._write(a, jnp.maximum(step - 1, 0)).wait()
        self.out_bufs[a][...] = self.in_bufs[a][...].astype(BF16)
        self._write(a, step).start()
        self._read(a, jnp.minimum(step + 1, n_steps - 1)).start()

    def drain(self, step):
        for a in range(len(self)):
            self._read(a, step).wait()
            self._write(a, step).wait()


def _slab_cast_specs(mats_f32, shapes, n_steps):
    slabs = []
    for w, (rows, cols) in zip(mats_f32, shapes):
        assert w.shape[1:] == (rows, cols) and rows % (n_steps * BF16_SUBLANE_TILE) == 0
        slabs.append((rows // n_steps, cols))
    out_shapes = [jax.ShapeDtypeStruct(shape, BF16) for shape in shapes[:len(mats_f32)]]
    scratch = [pltpu.VMEM(slab, F32) for slab in slabs] + [pltpu.VMEM(slab, BF16) for slab in slabs]
    if slabs:
        scratch.append(pltpu.SemaphoreType.DMA((2, len(slabs))))
    return out_shapes, scratch


def _in_proj_kernel(cast_layer, n_cast, *refs):
    x_ref, g_ref, cos_ref, sa_ref, sb_ref, w_ref = refs[:6]
    src_refs = refs[6:6 + n_cast]
    q_ref, kv_ref, c_ref, gu_ref = refs[6 + n_cast:10 + n_cast]
    dst_refs = refs[10 + n_cast:10 + 2 * n_cast]
    h_scr = refs[10 + 2 * n_cast]
    in_bufs = refs[11 + 2 * n_cast:11 + 3 * n_cast]
    out_bufs = refs[11 + 3 * n_cast:11 + 4 * n_cast]
    caster = _SlabCaster(cast_layer, src_refs, dst_refs, in_bufs, out_bufs,
                         refs[11 + 4 * n_cast] if n_cast else None)
    step, n_steps = pl.program_id(0), pl.num_programs(0)
    if n_cast:
        pl.when(step == 0)(caster.prime)
    tm = x_ref.shape[0]

    def rope(t, rows):
        return (t * cos_ref[rows, :] + pltpu.roll(t, HEAD_DIM - ROT_HALF, 1) * sa_ref[rows, :]
                + pltpu.roll(t, ROT_HALF, 1) * sb_ref[rows, :])

    def proj(lo, hi, rows=slice(None)):
        return jnp.dot(h_scr[rows, :], w_ref[:, lo:hi], preferred_element_type=F32)

    for half in range(2):
        rows = slice(half * tm // 2, (half + 1) * tm // 2)
        h_scr[rows, :] = _rms_norm_rows(x_ref[rows, :], g_ref[...]).astype(BF16)
        zq = proj(0, K_START, rows)
        for hd in range(N_Q_HEADS):
            sl = slice(hd * HEAD_DIM, (hd + 1) * HEAD_DIM)
            q_ref[rows, sl] = (rope(zq[:, sl], rows) * SCORE_SCALE_LOG2).astype(BF16)
    zk = proj(K_START, V_START)
    for hd in range(N_KV_HEADS):
        sl = slice(hd * HEAD_DIM, (hd + 1) * HEAD_DIM)
        kv_ref[:, sl] = rope(zk[:, sl], slice(None)).astype(BF16)
    kv_ref[:, KV_WIDTH:] = proj(V_START, CA_START).astype(BF16)
    za = proj(CA_START, CG_START)
    zg = proj(CG_START, UV_START)
    c_ref[...] = (za * jax.nn.sigmoid(zg)).astype(BF16)
    zu = proj(UV_START, IN_WIDTH)
    gelu = 0.5 * zu * (1.0 + lax.erf(zu * np.float32(np.sqrt(0.5))))
    gu_ref[...] = gelu.astype(BF16)

    for a in range(n_cast):
        caster.cast(a, step, n_steps)
    if n_cast:
        pl.when(step == n_steps - 1)(lambda: caster.drain(step))


def _in_proj(layer, x, norm_g, ropes, w_in, seq, cast_f32=()):
    tokens = x.shape[0]
    tm = TM_PROJ
    n_steps = tokens // tm
    seq_blocks = seq // tm
    row = lambda i: (i, 0)
    hbm = pl.BlockSpec(memory_space=pl.ANY)
    rope_spec = pl.BlockSpec((tm, HEAD_DIM), lambda i: (i % seq_blocks, 0))
    out_w = (ATTN_WIDTH, 2 * KV_WIDTH, CONV_WIDTH, 2 * SGU_WIDTH)
    cast_shapes, cast_scratch = _slab_cast_specs(cast_f32, NEXT_WEIGHT_SHAPES[1:], n_steps)
    outs = pl.pallas_call(
        functools.partial(_in_proj_kernel, layer, len(cast_f32)),
        grid=(n_steps,),
        in_specs=[
            pl.BlockSpec((tm, D_MODEL), row),
            pl.BlockSpec((None, 1, D_MODEL), lambda i: (layer, 0, 0)),
            rope_spec, rope_spec, rope_spec,
            pl.BlockSpec((D_MODEL, IN_WIDTH), lambda i: (0, 0), pipeline_mode=pl.Buffered(1)),
        ] + [hbm] * len(cast_f32),
        out_specs=[pl.BlockSpec((tm, w), row) for w in out_w] + [hbm] * len(cast_f32),
        out_shape=[jax.ShapeDtypeStruct((tokens, w), BF16) for w in out_w] + cast_shapes,
        scratch_shapes=[pltpu.VMEM((tm, D_MODEL), BF16)] + cast_scratch,
        compiler_params=pltpu.CompilerParams(
            dimension_semantics=("arbitrary",), vmem_limit_bytes=V7X_VMEM_LIMIT_BYTES),
        name=f"in_proj_l{layer}",
    )(x, norm_g, *ropes, w_in, *cast_f32)
    return tuple(outs[:len(out_w)]), tuple(outs[len(out_w):])


def _mixer_kernel(layer, sink_ref, x_ref, q_ref, kvc_ref, kvp_ref, kvn_ref, cc_ref, cp_ref, cn_ref,
                  gu_ref, dww_ref, dwb_ref, clg_ref, clb_ref, slg_ref, slb_ref, sw_ref, sbt_ref, wo_ref,
                  o_ref, mix_scr, kv_scr, c_scr, wb_scr, y_scr):
    tb = q_ref.shape[0]
    conv_col = ATTN_WIDTH
    sgu_col = ATTN_WIDTH + CONV_WIDTH

    i = pl.program_id(1)
    is_first = (i == 0).astype(jnp.int32)
    is_last = (i == pl.num_programs(1) - 1).astype(jnp.int32)

    kv_scr[0:BLOCK, :] = kvp_ref[...]
    kv_scr[BLOCK:BLOCK + tb, :] = kvc_ref[...]
    kv_scr[BLOCK + tb:, :] = kvn_ref[...]
    r = lax.broadcasted_iota(jnp.int32, (BLOCK, BLOCK), 0)
    c = lax.broadcasted_iota(jnp.int32, (BLOCK, BLOCK), 1)
    prev_bias = jnp.where(c >= r, 0.0, MASK_VALUE).astype(F32)
    next_bias = jnp.where(c <= r, 0.0, MASK_VALUE).astype(F32)
    prev_bias_first = jnp.minimum(prev_bias, MASK_VALUE * is_first.astype(F32))
    next_bias_last = jnp.minimum(next_bias, MASK_VALUE * is_last.astype(F32))
    n_sub = tb // BLOCK
    for j in range(n_sub):
        lo_bias = prev_bias_first if j == 0 else prev_bias
        hi_bias = next_bias_last if j == n_sub - 1 else next_bias
        rows = slice(j * BLOCK, (j + 1) * BLOCK)
        win = slice(j * BLOCK, (j + 3) * BLOCK)
        for g in range(N_KV_HEADS):
            heads = range(g * Q_PER_KV, (g + 1) * Q_PER_KV)
            qg = jnp.concatenate(
                [q_ref[rows, hd * HEAD_DIM:(hd + 1) * HEAD_DIM] for hd in heads], axis=0)
            kwin = kv_scr[win, g * HEAD_DIM:(g + 1) * HEAD_DIM]
            vwin = kv_scr[win, KV_WIDTH + g * HEAD_DIM:KV_WIDTH + (g + 1) * HEAD_DIM]
            s = lax.dot_general(qg, kwin, (((1,), (1,)), ((), ())), preferred_element_type=F32)
            probs, inv_l = [], []
            for hh, hd in enumerate(heads):
                sink = sink_ref[layer, hd] * LOG2_E
                hrows = slice(hh * BLOCK, (hh + 1) * BLOCK)
                sh = jnp.concatenate([s[hrows, 0:BLOCK] + lo_bias, s[hrows, BLOCK:2 * BLOCK],
                                      s[hrows, 2 * BLOCK:] + hi_bias], axis=1)
                m = jnp.maximum(jnp.max(sh, axis=-1, keepdims=True), sink)
                p = jnp.exp2(sh - m)
                denom = jnp.sum(p, axis=-1, keepdims=True) + jnp.exp2(sink - m)
                probs.append(p.astype(BF16))
                inv_l.append(1.0 / denom)
            o = jnp.dot(jnp.concatenate(probs, axis=0), vwin, preferred_element_type=F32)
            for hh, hd in enumerate(heads):
                oh = o[hh * BLOCK:(hh + 1) * BLOCK, :] * inv_l[hh]
                mix_scr[rows, hd * HEAD_DIM:(hd + 1) * HEAD_DIM] = oh.astype(BF16)

    for ci in range(tb // CHUNK):
        rows = slice(ci * CHUNK, (ci + 1) * CHUNK)
        v = _layer_norm_rows(gu_ref[rows, SGU_WIDTH:].astype(F32), slg_ref[...], slb_ref[...])
        v = v.astype(BF16)
        for hd in range(SGU_HEADS):
            cols = slice(hd * HEAD_DIM, (hd + 1) * HEAD_DIM)
            sp = jnp.dot(sw_ref[hd], v[:, cols], preferred_element_type=F32) + sbt_ref[:, hd:hd + 1]
            u = gu_ref[rows, cols].astype(F32)
            mix_scr[rows, sgu_col + hd * HEAD_DIM:sgu_col + (hd + 1) * HEAD_DIM] = (u * sp).astype(BF16)

    halo = cp_ref.shape[0]
    c_scr[0, 0:halo, :] = cp_ref[...].astype(F32) * (1 - is_first).astype(F32)
    c_scr[0, halo:halo + tb, :] = cc_ref[...].astype(F32)
    c_scr[0, halo + tb:, :] = cn_ref[...].astype(F32) * (1 - is_last).astype(F32)
    shifted_rows = tb + 2 * halo - F32_SUBLANE_TILE
    for s in range(1, F32_SUBLANE_TILE):
        c_scr[s, 0:shifted_rows, :] = c_scr[0, s:s + shifted_rows, :]
    first_tap = halo - CONV_PAD

    for k in range(CONV_KERNEL):
        wb_scr[k] = jnp.broadcast_to(dww_ref[k:k + 1, :], (F32_SUBLANE_TILE, CONV_WIDTH))
    wb_scr[CONV_KERNEL] = jnp.broadcast_to(dwb_ref[...], (F32_SUBLANE_TILE, CONV_WIDTH))
    groups = CONV_ROWS // F32_SUBLANE_TILE

    def conv_rows(t, carry):
        r0 = pl.multiple_of(t * CONV_ROWS, CONV_ROWS)
        acc = [wb_scr[CONV_KERNEL]] * groups
        for k in range(CONV_KERNEL):
            shift = (first_tap + k) % F32_SUBLANE_TILE
            w = wb_scr[k]
            for gi in range(groups):
                start = pl.multiple_of(
                    r0 + (first_tap + k - shift + gi * F32_SUBLANE_TILE), F32_SUBLANE_TILE)
                acc[gi] = acc[gi] + c_scr[shift, pl.ds(start, F32_SUBLANE_TILE), :] * w
        for gi in range(groups):
            out_row = pl.multiple_of(r0 + gi * F32_SUBLANE_TILE, F32_SUBLANE_TILE)
            y_scr[pl.ds(out_row, F32_SUBLANE_TILE), :] = acc[gi]
        return carry

    lax.fori_loop(0, tb // CONV_ROWS, conv_rows, 0, unroll=2)
    y = _layer_norm_rows(y_scr[...], clg_ref[...], clb_ref[...])
    mix_scr[:, conv_col:sgu_col] = (y * jax.nn.sigmoid(y)).astype(BF16)

    o_ref[...] = x_ref[...] + jnp.dot(mix_scr[...], wo_ref[...], preferred_element_type=F32)


def _mixer(layer, sink, x, q, kv, c, gu, dww, dwb, clg, clb, slg, slb, sgu_w, sgu_bt, w_out):
    batch, seq, _ = q.shape
    tb = TB_MIX
    halo = BF16_SUBLANE_TILE
    assert CONV_PAD <= halo and tb % BLOCK == 0 and tb % CONV_ROWS == 0
    kv_per = tb // BLOCK
    c_per = tb // halo
    cur = lambda b, i: (b, i, 0)
    kv_prev = lambda b, i: (b, jnp.maximum(i * kv_per - 1, 0), 0)
    kv_next = lambda b, i: (b, jnp.minimum((i + 1) * kv_per, seq // BLOCK - 1), 0)
    c_prev = lambda b, i: (b, jnp.maximum(i * c_per - 1, 0), 0)
    c_next = lambda b, i: (b, jnp.minimum((i + 1) * c_per, seq // halo - 1), 0)
    per_layer = lambda b, i: (layer, 0, 0)
    vec = lambda w: pl.BlockSpec((None, 1, w), per_layer)
    return pl.pallas_call(
        functools.partial(_mixer_kernel, layer),
        grid=(batch, seq // tb),
        in_specs=[
            pl.BlockSpec(memory_space=pltpu.SMEM),
            pl.BlockSpec((None, tb, D_MODEL), cur),
            pl.BlockSpec((None, tb, ATTN_WIDTH), cur),
            pl.BlockSpec((None, tb, 2 * KV_WIDTH), cur),
            pl.BlockSpec((None, BLOCK, 2 * KV_WIDTH), kv_prev),
            pl.BlockSpec((None, BLOCK, 2 * KV_WIDTH), kv_next),
            pl.BlockSpec((None, tb, CONV_WIDTH), cur),
            pl.BlockSpec((None, halo, CONV_WIDTH), c_prev),
            pl.BlockSpec((None, halo, CONV_WIDTH), c_next),
            pl.BlockSpec((None, tb, 2 * SGU_WIDTH), cur),
            pl.BlockSpec((None, CONV_KERNEL, CONV_WIDTH), per_layer),
            vec(CONV_WIDTH), vec(CONV_WIDTH), vec(CONV_WIDTH), vec(SGU_WIDTH), vec(SGU_WIDTH),
            pl.BlockSpec((None, SGU_HEADS, CHUNK, CHUNK), lambda b, i: (layer, 0, 0, 0)),
            pl.BlockSpec((None, CHUNK, SGU_HEADS), per_layer),
            pl.BlockSpec((D_MODEL, D_MODEL), lambda b, i: (0, 0), pipeline_mode=pl.Buffered(1)),
        ],
        out_specs=pl.BlockSpec((None, tb, D_MODEL), cur),
        out_shape=jax.ShapeDtypeStruct((batch, seq, D_MODEL), F32),
        scratch_shapes=[
            pltpu.VMEM((tb, D_MODEL), BF16),
            pltpu.VMEM((tb + 2 * BLOCK, 2 * KV_WIDTH), BF16),
            pltpu.VMEM((F32_SUBLANE_TILE, tb + 2 * halo, CONV_WIDTH), F32),
            pltpu.VMEM((CONV_KERNEL + 1, F32_SUBLANE_TILE, CONV_WIDTH), F32),
            pltpu.VMEM((tb, CONV_WIDTH), F32),
        ],
        compiler_params=pltpu.CompilerParams(
            dimension_semantics=("arbitrary", "arbitrary"), vmem_limit_bytes=V7X_VMEM_LIMIT_BYTES),
        name=f"mixer_l{layer}",
    )(sink, x, q, kv, kv, kv, c, c, c, gu, dww, dwb, clg, clb, slg, slb, sgu_w, sgu_bt, w_out)


def _ffn_kernel(next_layer, apply_final_norm, *refs):
    n_mats = len(NEXT_WEIGHT_SHAPES) if next_layer is not None else 0
    x_ref, g_ref, fg_ref, wg_hbm, wu_hbm, wd_hbm = refs[:6]
    o_ref = refs[6 + n_mats]
    h_scr, wgu_buf, wd_buf, sems = refs[7 + 2 * n_mats:11 + 2 * n_mats]
    caster = _SlabCaster(
        next_layer, refs[6:6 + n_mats], refs[7 + n_mats:7 + 2 * n_mats],
        refs[11 + 2 * n_mats:11 + 3 * n_mats], refs[11 + 3 * n_mats:11 + 4 * n_mats],
        refs[11 + 4 * n_mats] if n_mats else None)

    i = pl.program_id(0)
    n_tiles = pl.num_programs(0)
    chunks = _ffn_chunks()
    n_chunks = len(chunks)

    def gate_up_copies(c):
        slot, (lo, width) = c % FFN_SLOTS, chunks[c]
        return (
            pltpu.make_async_copy(wg_hbm.at[:, pl.ds(lo, width)],
                                  wgu_buf.at[slot, :, pl.ds(0, width)], sems.at[0, slot]),
            pltpu.make_async_copy(wu_hbm.at[:, pl.ds(lo, width)],
                                  wgu_buf.at[slot, :, pl.ds(width, width)], sems.at[1, slot]),
        )

    def down_copies(c):
        slot, (lo, width) = c % FFN_SLOTS, chunks[c]
        return (
            pltpu.make_async_copy(wd_hbm.at[pl.ds(lo, width), :],
                                  wd_buf.at[slot, pl.ds(0, width), :], sems.at[2, slot]),
        )

    def start(copies_of, c):
        for copy in copies_of(c % n_chunks):
            copy.start()

    def wait(copies_of, c):
        for copy in copies_of(c):
            copy.wait()

    @pl.when(i == 0)
    def _():
        start(gate_up_copies, 0)
        start(gate_up_copies, 1)
        start(down_copies, 0)
        caster.prime()

    tm = x_ref.shape[0]
    wait(gate_up_copies, 0)

    def gate_up_act(rows, slot, width):
        gate_up = jnp.dot(h_scr[rows, :], wgu_buf[slot, :, 0:2 * width], preferred_element_type=F32)
        gate, up = gate_up[:, 0:width], gate_up[:, width:]
        return (gate * jax.nn.sigmoid(gate) * up).astype(BF16)

    first_groups = [slice(g * tm // FFN_ROW_SPLIT, (g + 1) * tm // FFN_ROW_SPLIT)
                    for g in range(FFN_ROW_SPLIT)]
    for c, (_, width) in enumerate(chunks):
        slot = c % FFN_SLOTS
        halves = first_groups if c == 0 else [slice(0, tm)]
        acts = []
        for rows in halves:
            if c == 0:
                x = x_ref[rows, :]
                o_ref[rows, :] = x
                h_scr[rows, :] = _rms_norm_rows(x, g_ref[...]).astype(BF16)
            acts.append(gate_up_act(rows, slot, width))
        start(gate_up_copies, c + 2)
        start(down_copies, c + 1)
        if c + 1 < n_chunks:
            wait(gate_up_copies, c + 1)
        wait(down_copies, c)
        if c < n_mats:
            caster.cast(c, i, n_tiles)
        for rows, act in zip(halves, acts):
            o_ref[rows, :] += jnp.dot(act, wd_buf[slot, 0:width, :], preferred_element_type=F32)

    if apply_final_norm:
        o_ref[...] = _rms_norm_rows(o_ref[...], fg_ref[...])

    @pl.when(i == n_tiles - 1)
    def _():
        wait(gate_up_copies, 0)
        wait(gate_up_copies, 1)
        wait(down_copies, 0)
        caster.drain(i)


def _ffn_chunks():
    bounds = list(range(0, D_FF, TF_FFN)) + [D_FF]
    return tuple((lo, hi - lo) for lo, hi in zip(bounds[:-1], bounds[1:]))


def _ffn(layer, x, norm_g, w_gate, w_up, w_down, final_g, apply_final_norm, next_f32=()):
    tokens = x.shape[0]
    tm, tf = TM_FFN, TF_FFN
    n_tiles = tokens // tm
    assert FFN_SLOTS == 2 and len(_ffn_chunks()) % FFN_SLOTS == 0
    assert len(next_f32) in (0, len(NEXT_WEIGHT_SHAPES)) and len(next_f32) <= len(_ffn_chunks())
    cast_shapes, cast_scratch = _slab_cast_specs(next_f32, NEXT_WEIGHT_SHAPES, n_tiles)
    row = lambda i: (i, 0)
    hbm = pl.BlockSpec(memory_space=pl.ANY)
    outs = pl.pallas_call(
        functools.partial(_ffn_kernel, layer + 1 if next_f32 else None, apply_final_norm),
        grid=(n_tiles,),
        in_specs=[
            pl.BlockSpec((tm, D_MODEL), row),
            pl.BlockSpec((None, 1, D_MODEL), lambda i: (layer, 0, 0)),
            pl.BlockSpec((1, D_MODEL), lambda i: (0, 0)),
            hbm, hbm, hbm,
        ] + [hbm] * len(next_f32),
        out_specs=[pl.BlockSpec((tm, D_MODEL), row)] + [hbm] * len(next_f32),
        out_shape=[jax.ShapeDtypeStruct((tokens, D_MODEL), F32)] + cast_shapes,
        scratch_shapes=[
            pltpu.VMEM((tm, D_MODEL), BF16),
            pltpu.VMEM((FFN_SLOTS, D_MODEL, 2 * tf), BF16),
            pltpu.VMEM((FFN_SLOTS, tf, D_MODEL), BF16),
            pltpu.SemaphoreType.DMA((3, FFN_SLOTS)),
        ] + cast_scratch,
        compiler_params=pltpu.CompilerParams(
            dimension_semantics=("arbitrary",), vmem_limit_bytes=V7X_VMEM_LIMIT_BYTES),
        name=f"ffn_l{layer}",
    )(x, norm_g, final_g, w_gate, w_up, w_down, *next_f32)
    return outs[0], tuple(outs[1:])


@jax.jit
def _forward(x, mix_norm_g, w_in, sink, conv_dw_w, conv_dw_b, conv_ln_g, conv_ln_b,
             sgu_ln_g, sgu_ln_b, sgu_w, sgu_b, w_out, ffn_norm_g, w_gate, w_up, w_down,
             final_norm_g):
    batch, seq, d = x.shape
    tokens = batch * seq
    ropes = _rope_tables(seq)
    weights_f32 = (w_in, w_out, w_gate, w_up, w_down)
    weights = (w_in[0].astype(BF16),)
    sgu_w = sgu_w.astype(BF16)
    vec3 = lambda p: p.reshape(DEPTH, 1, p.shape[-1])
    mix_norm_g, ffn_norm_g = vec3(mix_norm_g), vec3(ffn_norm_g)
    conv_dw_b, conv_ln_g, conv_ln_b = vec3(conv_dw_b), vec3(conv_ln_g), vec3(conv_ln_b)
    sgu_ln_g, sgu_ln_b = vec3(sgu_ln_g), vec3(sgu_ln_b)
    sgu_bt = jnp.swapaxes(sgu_b, 1, 2)
    final_g = final_norm_g.reshape(1, d)

    xf = x.reshape(tokens, d)
    for layer in range(DEPTH):
        last = layer == DEPTH - 1
        (q, kv, c, gu), cast_now = _in_proj(layer, xf, mix_norm_g, ropes, weights[0], seq,
                                            cast_f32=weights_f32[1:] if layer == 0 else ())
        _, w_out_l, w_gate_l, w_up_l, w_down_l = weights + cast_now
        to3 = lambda a: a.reshape(batch, seq, a.shape[-1])
        xn = _mixer(layer, sink, to3(xf), to3(q), to3(kv), to3(c), to3(gu), conv_dw_w, conv_dw_b,
                    conv_ln_g, conv_ln_b, sgu_ln_g, sgu_ln_b, sgu_w, sgu_bt, w_out_l)
        xf, weights = _ffn(layer, xn.reshape(tokens, d), ffn_norm_g, w_gate_l, w_up_l, w_down_l,
                           final_g, apply_final_norm=last, next_f32=() if last else weights_f32)
    return xf.reshape(batch, seq, d)


def kernel(x, mix_norm_g, w_in, sink, conv_dw_w, conv_dw_b, conv_ln_g, conv_ln_b, sgu_ln_g, sgu_ln_b,
           sgu_w, sgu_b, w_out, ffn_norm_g, w_gate, w_up, w_down, final_norm_g):
    return _forward(x, mix_norm_g, w_in, sink, conv_dw_w, conv_dw_b, conv_ln_g, conv_ln_b,
                    sgu_ln_g, sgu_ln_b, sgu_w, sgu_b, w_out, ffn_norm_g, w_gate, w_up, w_down,
                    final_norm_g)
```

```python
import functools

import jax
import jax.numpy as jnp
import numpy as np
from jax import lax
from jax.experimental import pallas as pl
from jax.experimental.pallas import tpu as pltpu

F32 = jnp.float32
BF16 = jnp.bfloat16

D_MODEL = 2048
DEPTH = 4
HEAD_DIM = 128
ATTN_WIDTH = D_MODEL // 2
N_Q_HEADS = ATTN_WIDTH // HEAD_DIM
N_KV_HEADS = N_Q_HEADS // 4
Q_PER_KV = N_Q_HEADS // N_KV_HEADS
KV_WIDTH = N_KV_HEADS * HEAD_DIM
CONV_WIDTH = D_MODEL // 4
CONV_KERNEL = 31
CONV_PAD = (CONV_KERNEL - 1) // 2
SGU_WIDTH = D_MODEL // 4
SGU_HEADS = SGU_WIDTH // HEAD_DIM
CHUNK = 128
IN_WIDTH = ATTN_WIDTH + 2 * KV_WIDTH + 2 * CONV_WIDTH + 2 * SGU_WIDTH
WINDOW = 128
BLOCK = 128
ROPE_THETA = 500000.0
ROT_DIM = HEAD_DIM // 4
ROT_HALF = ROT_DIM // 2
D_FF = ((8 * D_MODEL // 3 + 255) // 256) * 256
EPS = 1e-6

NEXT_WEIGHT_SHAPES = ((D_MODEL, IN_WIDTH), (D_MODEL, D_MODEL), (D_MODEL, D_FF), (D_MODEL, D_FF),
                      (D_FF, D_MODEL))

K_START = ATTN_WIDTH
V_START = K_START + KV_WIDTH
CA_START = V_START + KV_WIDTH
CG_START = CA_START + CONV_WIDTH
UV_START = CG_START + CONV_WIDTH

V7X_VMEM_LIMIT_BYTES = 60 * 1024 * 1024
BF16_SUBLANE_TILE = 16
F32_SUBLANE_TILE = 8
TM_PROJ = 512
TB_MIX = 512
TM_FFN = 512
TF_FFN = 1024
FFN_SLOTS = 2
FFN_ROW_SPLIT = 2
CONV_ROWS = 32
MASK_VALUE = -1e30
LOG2_E = np.float32(np.log2(np.e))
SCORE_SCALE_LOG2 = np.float32(np.log2(np.e) / np.sqrt(HEAD_DIM))


def _rms_norm_rows(x, g):
    ms = jnp.mean(x * x, axis=-1, keepdims=True)
    return x * lax.rsqrt(ms + EPS) * g


def _layer_norm_rows(x, g, b):
    mu = jnp.mean(x, axis=-1, keepdims=True)
    xc = x - mu
    var = jnp.mean(xc * xc, axis=-1, keepdims=True)
    return xc * lax.rsqrt(var + EPS) * g + b


def _rope_tables(seq):
    pos = jnp.arange(seq, dtype=F32)
    inv = ROPE_THETA ** (-jnp.arange(0, ROT_DIM, 2, dtype=F32) / ROT_DIM)
    ang = pos[:, None] * inv[None, :]
    cos, sin = jnp.cos(ang), jnp.sin(ang)
    rest = HEAD_DIM - ROT_DIM
    cos_t = jnp.concatenate([cos, cos, jnp.ones((seq, rest), F32)], axis=-1)
    sin_a = jnp.concatenate([-sin, jnp.zeros((seq, HEAD_DIM - ROT_HALF), F32)], axis=-1)
    sin_b = jnp.concatenate([jnp.zeros((seq, ROT_HALF), F32), sin, jnp.zeros((seq, rest), F32)], axis=-1)
    return cos_t, sin_a, sin_b


class _SlabCaster:
    def __init__(self, layer, src_refs, dst_refs, in_bufs, out_bufs, sems):
        self.layer, self.src_refs, self.dst_refs = layer, src_refs, dst_refs
        self.in_bufs, self.out_bufs, self.sems = in_bufs, out_bufs, sems

    def __len__(self):
        return len(self.src_refs)

    def _read(self, a, step):
        rows = self.in_bufs[a].shape[0]
        return pltpu.make_async_copy(self.src_refs[a].at[self.layer, pl.ds(step * rows, rows), :],
                                     self.in_bufs[a], self.sems.at[0, a])

    def _write(self, a, step):
        rows = self.out_bufs[a].shape[0]
        return pltpu.make_async_copy(self.out_bufs[a], self.dst_refs[a].at[pl.ds(step * rows, rows), :],
                                     self.sems.at[1, a])

    def prime(self):
        for a in range(len(self)):
            self._read(a, 0).start()
            self.out_bufs[a][...] = jnp.zeros(self.out_bufs[a].shape, BF16)
            self._write(a, 0).start()

    def cast(self, a, step, n_steps):
        self._read(a, step).wait()
        self._write(a, jnp.maximum(step - 1, 0)).wait()
        self.out_bufs[a][...] = self.in_bufs[a][...].astype(BF16)
        self._write(a, step).start()
        self._read(a, jnp.minimum(step + 1, n_steps - 1)).start()

    def drain(self, step):
        for a in range(len(self)):
            self._read(a, step).wait()
            self._write(a, step).wait()


def _slab_cast_specs(mats_f32, shapes, n_steps):
    slabs = []
    for w, (rows, cols) in zip(mats_f32, shapes):
        assert w.shape[1:] == (rows, cols) and rows % (n_steps * BF16_SUBLANE_TILE) == 0
        slabs.append((rows // n_steps, cols))
    out_shapes = [jax.ShapeDtypeStruct(shape, BF16) for shape in shapes[:len(mats_f32)]]
    scratch = [pltpu.VMEM(slab, F32) for slab in slabs] + [pltpu.VMEM(slab, BF16) for slab in slabs]
    if slabs:
        scratch.append(pltpu.SemaphoreType.DMA((2, len(slabs))))
    return out_shapes, scratch


def _in_proj_kernel(cast_layer, n_cast, *refs):
    x_ref, g_ref, cos_ref, sa_ref, sb_ref, w_ref = refs[:6]
    src_refs = refs[6:6 + n_cast]
    q_ref, kv_ref, c_ref, gu_ref = refs[6 + n_cast:10 + n_cast]
    dst_refs = refs[10 + n_cast:10 + 2 * n_cast]
    h_scr = refs[10 + 2 * n_cast]
    in_bufs = refs[11 + 2 * n_cast:11 + 3 * n_cast]
    out_bufs = refs[11 + 3 * n_cast:11 + 4 * n_cast]
    caster = _SlabCaster(cast_layer, src_refs, dst_refs, in_bufs, out_bufs,
                         refs[11 + 4 * n_cast] if n_cast else None)
    step, n_steps = pl.program_id(0), pl.num_programs(0)
    if n_cast:
        pl.when(step == 0)(caster.prime)
    tm = x_ref.shape[0]

    def rope(t, rows):
        return (t * cos_ref[rows, :] + pltpu.roll(t, HEAD_DIM - ROT_HALF, 1) * sa_ref[rows, :]
                + pltpu.roll(t, ROT_HALF, 1) * sb_ref[rows, :])

    def proj(lo, hi, rows=slice(None)):
        return jnp.dot(h_scr[rows, :], w_ref[:, lo:hi], preferred_element_type=F32)

    for half in range(2):
        rows = slice(half * tm // 2, (half + 1) * tm // 2)
        h_scr[rows, :] = _rms_norm_rows(x_ref[rows, :], g_ref[...]).astype(BF16)
        zq = proj(0, K_START, rows)
        for hd in range(N_Q_HEADS):
            sl = slice(hd * HEAD_DIM, (hd + 1) * HEAD_DIM)
            q_ref[rows, sl] = (rope(zq[:, sl], rows) * SCORE_SCALE_LOG2).astype(BF16)
    zk = proj(K_START, V_START)
    for hd in range(N_KV_HEADS):
        sl = slice(hd * HEAD_DIM, (hd + 1) * HEAD_DIM)
        kv_ref[:, sl] = rope(zk[:, sl], slice(None)).astype(BF16)
    kv_ref[:, KV_WIDTH:] = proj(V_START, CA_START).astype(BF16)
    za = proj(CA_START, CG_START)
    zg = proj(CG_START, UV_START)
    c_ref[...] = (za * jax.nn.sigmoid(zg)).astype(BF16)
    zu = proj(UV_START, IN_WIDTH)
    gelu = 0.5 * zu * (1.0 + lax.erf(zu * np.float32(np.sqrt(0.5))))
    gu_ref[...] = gelu.astype(BF16)

    for a in range(n_cast):
        caster.cast(a, step, n_steps)
    if n_cast:
        pl.when(step == n_steps - 1)(lambda: caster.drain(step))


def _in_proj(layer, x, norm_g, ropes, w_in, seq, cast_f32=()):
    tokens = x.shape[0]
    tm = TM_PROJ
    n_steps = tokens // tm
    seq_blocks = seq // tm
    row = lambda i: (i, 0)
    hbm = pl.BlockSpec(memory_space=pl.ANY)
    rope_spec = pl.BlockSpec((tm, HEAD_DIM), lambda i: (i % seq_blocks, 0))
    out_w = (ATTN_WIDTH, 2 * KV_WIDTH, CONV_WIDTH, 2 * SGU_WIDTH)
    cast_shapes, cast_scratch = _slab_cast_specs(cast_f32, NEXT_WEIGHT_SHAPES[1:], n_steps)
    outs = pl.pallas_call(
        functools.partial(_in_proj_kernel, layer, len(cast_f32)),
        grid=(n_steps,),
        in_specs=[
            pl.BlockSpec((tm, D_MODEL), row),
            pl.BlockSpec((None, 1, D_MODEL), lambda i: (layer, 0, 0)),
            rope_spec, rope_spec, rope_spec,
            pl.BlockSpec((D_MODEL, IN_WIDTH), lambda i: (0, 0), pipeline_mode=pl.Buffered(1)),
        ] + [hbm] * len(cast_f32),
        out_specs=[pl.BlockSpec((tm, w), row) for w in out_w] + [hbm] * len(cast_f32),
        out_shape=[jax.ShapeDtypeStruct((tokens, w), BF16) for w in out_w] + cast_shapes,
        scratch_shapes=[pltpu.VMEM((tm, D_MODEL), BF16)] + cast_scratch,
        compiler_params=pltpu.CompilerParams(
            dimension_semantics=("arbitrary",), vmem_limit_bytes=V7X_VMEM_LIMIT_BYTES),
        name=f"in_proj_l{layer}",
    )(x, norm_g, *ropes, w_in, *cast_f32)
    return tuple(outs[:len(out_w)]), tuple(outs[len(out_w):])


def _mixer_kernel(layer, sink_ref, x_ref, q_ref, kvc_ref, kvp_ref, kvn_ref, cc_ref, cp_ref, cn_ref,
                  gu_ref, dww_ref, dwb_ref, clg_ref, clb_ref, slg_ref, slb_ref, sw_ref, sbt_ref, wo_ref,
                  o_ref, mix_scr, kv_scr, c_scr, wb_scr, y_scr):
    tb = q_ref.shape[0]
    conv_col = ATTN_WIDTH
    sgu_col = ATTN_WIDTH + CONV_WIDTH

    i = pl.program_id(1)
    is_first = (i == 0).astype(jnp.int32)
    is_last = (i == pl.num_programs(1) - 1).astype(jnp.int32)

    kv_scr[0:BLOCK, :] = kvp_ref[...]
    kv_scr[BLOCK:BLOCK + tb, :] = kvc_ref[...]
    kv_scr[BLOCK + tb:, :] = kvn_ref[...]
    r = lax.broadcasted_iota(jnp.int32, (BLOCK, BLOCK), 0)
    c = lax.broadcasted_iota(jnp.int32, (BLOCK, BLOCK), 1)
    prev_bias = jnp.where(c >= r, 0.0, MASK_VALUE).astype(F32)
    next_bias = jnp.where(c <= r, 0.0, MASK_VALUE).astype(F32)
    prev_bias_first = jnp.minimum(prev_bias, MASK_VALUE * is_first.astype(F32))
    next_bias_last = jnp.minimum(next_bias, MASK_VALUE * is_last.astype(F32))
    n_sub = tb // BLOCK
    for j in range(n_sub):
        lo_bias = prev_bias_first if j == 0 else prev_bias
        hi_bias = next_bias_last if j == n_sub - 1 else next_bias
        rows = slice(j * BLOCK, (j + 1) * BLOCK)
        win = slice(j * BLOCK, (j + 3) * BLOCK)
        for g in range(N_KV_HEADS):
            heads = range(g * Q_PER_KV, (g + 1) * Q_PER_KV)
            qg = jnp.concatenate(
                [q_ref[rows, hd * HEAD_DIM:(hd + 1) * HEAD_DIM] for hd in heads], axis=0)
            kwin = kv_scr[win, g * HEAD_DIM:(g + 1) * HEAD_DIM]
            vwin = kv_scr[win, KV_WIDTH + g * HEAD_DIM:KV_WIDTH + (g + 1) * HEAD_DIM]
            s = lax.dot_general(qg, kwin, (((1,), (1,)), ((), ())), preferred_element_type=F32)
            probs, inv_l = [], []
            for hh, hd in enumerate(heads):
                sink = sink_ref[layer, hd] * LOG2_E
                hrows = slice(hh * BLOCK, (hh + 1) * BLOCK)
                sh = jnp.concatenate([s[hrows, 0:BLOCK] + lo_bias, s[hrows, BLOCK:2 * BLOCK],
                                      s[hrows, 2 * BLOCK:] + hi_bias], axis=1)
                m = jnp.maximum(jnp.max(sh, axis=-1, keepdims=True), sink)
                p = jnp.exp2(sh - m)
                denom = jnp.sum(p, axis=-1, keepdims=True) + jnp.exp2(sink - m)
                probs.append(p.astype(BF16))
                inv_l.append(1.0 / denom)
            o = jnp.dot(jnp.concatenate(probs, axis=0), vwin, preferred_element_type=F32)
            for hh, hd in enumerate(heads):
                oh = o[hh * BLOCK:(hh + 1) * BLOCK, :] * inv_l[hh]
                mix_scr[rows, hd * HEAD_DIM:(hd + 1) * HEAD_DIM] = oh.astype(BF16)

    for ci in range(tb // CHUNK):
        rows = slice(ci * CHUNK, (ci + 1) * CHUNK)
        v = _layer_norm_rows(gu_ref[rows, SGU_WIDTH:].astype(F32), slg_ref[...], slb_ref[...])
        v = v.astype(BF16)
        for hd in range(SGU_HEADS):
            cols = slice(hd * HEAD_DIM, (hd + 1) * HEAD_DIM)
            sp = jnp.dot(sw_ref[hd], v[:, cols], preferred_element_type=F32) + sbt_ref[:, hd:hd + 1]
            u = gu_ref[rows, cols].astype(F32)
            mix_scr[rows, sgu_col + hd * HEAD_DIM:sgu_col + (hd + 1) * HEAD_DIM] = (u * sp).astype(BF16)

    halo = cp_ref.shape[0]
    c_scr[0, 0:halo, :] = cp_ref[...].astype(F32) * (1 - is_first).astype(F32)
    c_scr[0, halo:halo + tb, :] = cc_ref[...].astype(F32)
    c_scr[0, halo + tb:, :] = cn_ref[...].astype(F32) * (1 - is_last).astype(F32)
    shifted_rows = tb + 2 * halo - F32_SUBLANE_TILE
    for s in range(1, F32_SUBLANE_TILE):
        c_scr[s, 0:shifted_rows, :] = c_scr[0, s:s + shifted_rows, :]
    first_tap = halo - CONV_PAD

    for k in range(CONV_KERNEL):
        wb_scr[k] = jnp.broadcast_to(dww_ref[k:k + 1, :], (F32_SUBLANE_TILE, CONV_WIDTH))
    wb_scr[CONV_KERNEL] = jnp.broadcast_to(dwb_ref[...], (F32_SUBLANE_TILE, CONV_WIDTH))
    groups = CONV_ROWS // F32_SUBLANE_TILE

    def conv_rows(t, carry):
        r0 = pl.multiple_of(t * CONV_ROWS, CONV_ROWS)
        acc = [wb_scr[CONV_KERNEL]] * groups
        for k in range(CONV_KERNEL):
            shift = (first_tap + k) % F32_SUBLANE_TILE
            w = wb_scr[k]
            for gi in range(groups):
                start = pl.multiple_of(
                    r0 + (first_tap + k - shift + gi * F32_SUBLANE_TILE), F32_SUBLANE_TILE)
                acc[gi] = acc[gi] + c_scr[shift, pl.ds(start, F32_SUBLANE_TILE), :] * w
        for gi in range(groups):
            out_row = pl.multiple_of(r0 + gi * F32_SUBLANE_TILE, F32_SUBLANE_TILE)
            y_scr[pl.ds(out_row, F32_SUBLANE_TILE), :] = acc[gi]
        return carry

    lax.fori_loop(0, tb // CONV_ROWS, conv_rows, 0, unroll=4)
    y = _layer_norm_rows(y_scr[...], clg_ref[...], clb_ref[...])
    mix_scr[:, conv_col:sgu_col] = (y * jax.nn.sigmoid(y)).astype(BF16)

    o_ref[...] = x_ref[...] + jnp.dot(mix_scr[...], wo_ref[...], preferred_element_type=F32)


def _mixer(layer, sink, x, q, kv, c, gu, dww, dwb, clg, clb, slg, slb, sgu_w, sgu_bt, w_out):
    batch, seq, _ = q.shape
    tb = TB_MIX
    halo = BF16_SUBLANE_TILE
    assert CONV_PAD <= halo and tb % BLOCK == 0 and tb % CONV_ROWS == 0
    kv_per = tb // BLOCK
    c_per = tb // halo
    cur = lambda b, i: (b, i, 0)
    kv_prev = lambda b, i: (b, jnp.maximum(i * kv_per - 1, 0), 0)
    kv_next = lambda b, i: (b, jnp.minimum((i + 1) * kv_per, seq // BLOCK - 1), 0)
    c_prev = lambda b, i: (b, jnp.maximum(i * c_per - 1, 0), 0)
    c_next = lambda b, i: (b, jnp.minimum((i + 1) * c_per, seq // halo - 1), 0)
    per_layer = lambda b, i: (layer, 0, 0)
    vec = lambda w: pl.BlockSpec((None, 1, w), per_layer)
    return pl.pallas_call(
        functools.partial(_mixer_kernel, layer),
        grid=(batch, seq // tb),
        in_specs=[
            pl.BlockSpec(memory_space=pltpu.SMEM),
            pl.BlockSpec((None, tb, D_MODEL), cur),
            pl.BlockSpec((None, tb, ATTN_WIDTH), cur),
            pl.BlockSpec((None, tb, 2 * KV_WIDTH), cur),
            pl.BlockSpec((None, BLOCK, 2 * KV_WIDTH), kv_prev),
            pl.BlockSpec((None, BLOCK, 2 * KV_WIDTH), kv_next),
            pl.BlockSpec((None, tb, CONV_WIDTH), cur),
            pl.BlockSpec((None, halo, CONV_WIDTH), c_prev),
            pl.BlockSpec((None, halo, CONV_WIDTH), c_next),
            pl.BlockSpec((None, tb, 2 * SGU_WIDTH), cur),
            pl.BlockSpec((None, CONV_KERNEL, CONV_WIDTH), per_layer),
            vec(CONV_WIDTH), vec(CONV_WIDTH), vec(CONV_WIDTH), vec(SGU_WIDTH), vec(SGU_WIDTH),
            pl.BlockSpec((None, SGU_HEADS, CHUNK, CHUNK), lambda b, i: (layer, 0, 0, 0)),
            pl.BlockSpec((None, CHUNK, SGU_HEADS), per_layer),
            pl.BlockSpec((D_MODEL, D_MODEL), lambda b, i: (0, 0), pipeline_mode=pl.Buffered(1)),
        ],
        out_specs=pl.BlockSpec((None, tb, D_MODEL), cur),
        out_shape=jax.ShapeDtypeStruct((batch, seq, D_MODEL), F32),
        scratch_shapes=[
            pltpu.VMEM((tb, D_MODEL), BF16),
            pltpu.VMEM((tb + 2 * BLOCK, 2 * KV_WIDTH), BF16),
            pltpu.VMEM((F32_SUBLANE_TILE, tb + 2 * halo, CONV_WIDTH), F32),
            pltpu.VMEM((CONV_KERNEL + 1, F32_SUBLANE_TILE, CONV_WIDTH), F32),
            pltpu.VMEM((tb, CONV_WIDTH), F32),
        ],
        compiler_params=pltpu.CompilerParams(
            dimension_semantics=("arbitrary", "arbitrary"), vmem_limit_bytes=V7X_VMEM_LIMIT_BYTES),
        name=f"mixer_l{layer}",
    )(sink, x, q, kv, kv, kv, c, c, c, gu, dww, dwb, clg, clb, slg, slb, sgu_w, sgu_bt, w_out)


def _ffn_kernel(next_layer, apply_final_norm, *refs):
    n_mats = len(NEXT_WEIGHT_SHAPES) if next_layer is not None else 0
    x_ref, g_ref, fg_ref, wg_hbm, wu_hbm, wd_hbm = refs[:6]
    o_ref = refs[6 + n_mats]
    h_scr, wgu_buf, wd_buf, sems = refs[7 + 2 * n_mats:11 + 2 * n_mats]
    caster = _SlabCaster(
        next_layer, refs[6:6 + n_mats], refs[7 + n_mats:7 + 2 * n_mats],
        refs[11 + 2 * n_mats:11 + 3 * n_mats], refs[11 + 3 * n_mats:11 + 4 * n_mats],
        refs[11 + 4 * n_mats] if n_mats else None)

    i = pl.program_id(0)
    n_tiles = pl.num_programs(0)
    chunks = _ffn_chunks()
    n_chunks = len(chunks)

    def gate_up_copies(c):
        slot, (lo, width) = c % FFN_SLOTS, chunks[c]
        return (
            pltpu.make_async_copy(wg_hbm.at[:, pl.ds(lo, width)],
                                  wgu_buf.at[slot, :, pl.ds(0, width)], sems.at[0, slot]),
            pltpu.make_async_copy(wu_hbm.at[:, pl.ds(lo, width)],
                                  wgu_buf.at[slot, :, pl.ds(width, width)], sems.at[1, slot]),
        )

    def down_copies(c):
        slot, (lo, width) = c % FFN_SLOTS, chunks[c]
        return (
            pltpu.make_async_copy(wd_hbm.at[pl.ds(lo, width), :],
                                  wd_buf.at[slot, pl.ds(0, width), :], sems.at[2, slot]),
        )

    def start(copies_of, c):
        for copy in copies_of(c % n_chunks):
            copy.start()

    def wait(copies_of, c):
        for copy in copies_of(c):
            copy.wait()

    @pl.when(i == 0)
    def _():
        start(gate_up_copies, 0)
        start(gate_up_copies, 1)
        start(down_copies, 0)
        caster.prime()

    tm = x_ref.shape[0]
    wait(gate_up_copies, 0)

    def gate_up_act(rows, slot, width):
        gate_up = jnp.dot(h_scr[rows, :], wgu_buf[slot, :, 0:2 * width], preferred_element_type=F32)
        gate, up = gate_up[:, 0:width], gate_up[:, width:]
        return (gate * jax.nn.sigmoid(gate) * up).astype(BF16)

    first_groups = [slice(g * tm // FFN_ROW_SPLIT, (g + 1) * tm // FFN_ROW_SPLIT)
                    for g in range(FFN_ROW_SPLIT)]
    for c, (_, width) in enumerate(chunks):
        slot = c % FFN_SLOTS
        halves = first_groups if c == 0 else [slice(0, tm)]
        acts = []
        for rows in halves:
            if c == 0:
                x = x_ref[rows, :]
                o_ref[rows, :] = x
                h_scr[rows, :] = _rms_norm_rows(x, g_ref[...]).astype(BF16)
            acts.append(gate_up_act(rows, slot, width))
        start(gate_up_copies, c + 2)
        start(down_copies, c + 1)
        if c + 1 < n_chunks:
            wait(gate_up_copies, c + 1)
        wait(down_copies, c)
        if c < n_mats:
            caster.cast(c, i, n_tiles)
        for rows, act in zip(halves, acts):
            o_ref[rows, :] += jnp.dot(act, wd_buf[slot, 0:width, :], preferred_element_type=F32)

    if apply_final_norm:
        o_ref[...] = _rms_norm_rows(o_ref[...], fg_ref[...])

    @pl.when(i == n_tiles - 1)
    def _():
        wait(gate_up_copies, 0)
        wait(gate_up_copies, 1)
        wait(down_copies, 0)
        caster.drain(i)


def _ffn_chunks():
    bounds = list(range(0, D_FF, TF_FFN)) + [D_FF]
    return tuple((lo, hi - lo) for lo, hi in zip(bounds[:-1], bounds[1:]))


def _ffn(layer, x, norm_g, w_gate, w_up, w_down, final_g, apply_final_norm, next_f32=()):
    tokens = x.shape[0]
    tm, tf = TM_FFN, TF_FFN
    n_tiles = tokens // tm
    assert FFN_SLOTS == 2 and len(_ffn_chunks()) % FFN_SLOTS == 0
    assert len(next_f32) in (0, len(NEXT_WEIGHT_SHAPES)) and len(next_f32) <= len(_ffn_chunks())
    cast_shapes, cast_scratch = _slab_cast_specs(next_f32, NEXT_WEIGHT_SHAPES, n_tiles)
    row = lambda i: (i, 0)
    hbm = pl.BlockSpec(memory_space=pl.ANY)
    outs = pl.pallas_call(
        functools.partial(_ffn_kernel, layer + 1 if next_f32 else None, apply_final_norm),
        grid=(n_tiles,),
        in_specs=[
            pl.BlockSpec((tm, D_MODEL), row),
            pl.BlockSpec((None, 1, D_MODEL), lambda i: (layer, 0, 0)),
            pl.BlockSpec((1, D_MODEL), lambda i: (0, 0)),
            hbm, hbm, hbm,
        ] + [hbm] * len(next_f32),
        out_specs=[pl.BlockSpec((tm, D_MODEL), row)] + [hbm] * len(next_f32),
        out_shape=[jax.ShapeDtypeStruct((tokens, D_MODEL), F32)] + cast_shapes,
        scratch_shapes=[
            pltpu.VMEM((tm, D_MODEL), BF16),
            pltpu.VMEM((FFN_SLOTS, D_MODEL, 2 * tf), BF16),
            pltpu.VMEM((FFN_SLOTS, tf, D_MODEL), BF16),
            pltpu.SemaphoreType.DMA((3, FFN_SLOTS)),
        ] + cast_scratch,
        compiler_params=pltpu.CompilerParams(
            dimension_semantics=("arbitrary",), vmem_limit_bytes=V7X_VMEM_LIMIT_BYTES),
        name=f"ffn_l{layer}",
    )(x, norm_g, final_g, w_gate, w_up, w_down, *next_f32)
    return outs[0], tuple(outs[1:])


@jax.jit
def _forward(x, mix_norm_g, w_in, sink, conv_dw_w, conv_dw_b, conv_ln_g, conv_ln_b,
             sgu_ln_g, sgu_ln_b, sgu_w, sgu_b, w_out, ffn_norm_g, w_gate, w_up, w_down,
             final_norm_g):
    batch, seq, d = x.shape
    tokens = batch * seq
    ropes = _rope_tables(seq)
    weights_f32 = (w_in, w_out, w_gate, w_up, w_down)
    weights = (w_in[0].astype(BF16),)
    sgu_w = sgu_w.astype(BF16)
    vec3 = lambda p: p.reshape(DEPTH, 1, p.shape[-1])
    mix_norm_g, ffn_norm_g = vec3(mix_norm_g), vec3(ffn_norm_g)
    conv_dw_b, conv_ln_g, conv_ln_b = vec3(conv_dw_b), vec3(conv_ln_g), vec3(conv_ln_b)
    sgu_ln_g, sgu_ln_b = vec3(sgu_ln_g), vec3(sgu_ln_b)
    sgu_bt = jnp.swapaxes(sgu_b, 1, 2)
    final_g = final_norm_g.reshape(1, d)

    xf = x.reshape(tokens, d)
    for layer in range(DEPTH):
        last = layer == DEPTH - 1
        (q, kv, c, gu), cast_now = _in_proj(layer, xf, mix_norm_g, ropes, weights[0], seq,
                                            cast_f32=weights_f32[1:] if layer == 0 else ())
        _, w_out_l, w_gate_l, w_up_l, w_down_l = weights + cast_now
        to3 = lambda a: a.reshape(batch, seq, a.shape[-1])
        xn = _mixer(layer, sink, to3(xf), to3(q), to3(kv), to3(c), to3(gu), conv_dw_w, conv_dw_b,
                    conv_ln_g, conv_ln_b, sgu_ln_g, sgu_ln_b, sgu_w, sgu_bt, w_out_l)
        xf, weights = _ffn(layer, xn.reshape(tokens, d), ffn_norm_g, w_gate_l, w_up_l, w_down_l,
                           final_g, apply_final_norm=last, next_f32=() if last else weights_f32)
    return xf.reshape(batch, seq, d)


def kernel(x, mix_norm_g, w_in, sink, conv_dw_w, conv_dw_b, conv_ln_g, conv_ln_b, sgu_ln_g, sgu_ln_b,
           sgu_w, sgu_b, w_out, ffn_norm_g, w_gate, w_up, w_down, final_norm_g):
    return _forward(x, mix_norm_g, w_in, sink, conv_dw_w, conv_dw_b, conv_ln_g, conv_ln_b,
                    sgu_ln_g, sgu_ln_b, sgu_w, sgu_b, w_out, ffn_norm_g, w_gate, w_up, w_down,
                    final_norm_g)
```

```python
import functools

import jax
import jax.numpy as jnp
import numpy as np
from jax import lax
from jax.experimental import pallas as pl
from jax.experimental.pallas import tpu as pltpu

F32 = jnp.float32
BF16 = jnp.bfloat16

D_MODEL = 2048
DEPTH = 4
HEAD_DIM = 128
ATTN_WIDTH = D_MODEL // 2
N_Q_HEADS = ATTN_WIDTH // HEAD_DIM
N_KV_HEADS = N_Q_HEADS // 4
Q_PER_KV = N_Q_HEADS // N_KV_HEADS
KV_WIDTH = N_KV_HEADS * HEAD_DIM
CONV_WIDTH = D_MODEL // 4
CONV_KERNEL = 31
CONV_PAD = (CONV_KERNEL - 1) // 2
SGU_WIDTH = D_MODEL // 4
SGU_HEADS = SGU_WIDTH // HEAD_DIM
CHUNK = 128
IN_WIDTH = ATTN_WIDTH + 2 * KV_WIDTH + 2 * CONV_WIDTH + 2 * SGU_WIDTH
WINDOW = 128
BLOCK = 128
ROPE_THETA = 500000.0
ROT_DIM = HEAD_DIM // 4
ROT_HALF = ROT_DIM // 2
D_FF = ((8 * D_MODEL // 3 + 255) // 256) * 256
EPS = 1e-6

NEXT_WEIGHT_SHAPES = ((D_MODEL, IN_WIDTH), (D_MODEL, D_MODEL), (D_MODEL, D_FF), (D_MODEL, D_FF),
                      (D_FF, D_MODEL))

K_START = ATTN_WIDTH
V_START = K_START + KV_WIDTH
CA_START = V_START + KV_WIDTH
CG_START = CA_START + CONV_WIDTH
UV_START = CG_START + CONV_WIDTH

V7X_VMEM_LIMIT_BYTES = 60 * 1024 * 1024
BF16_SUBLANE_TILE = 16
F32_SUBLANE_TILE = 8
TM_PROJ = 512
TB_MIX = 512
TM_FFN = 512
TF_FFN = 1024
FFN_SLOTS = 2
FFN_ROW_SPLIT = 2
CONV_ROWS = 32
MASK_VALUE = -1e30
LOG2_E = np.float32(np.log2(np.e))
SCORE_SCALE_LOG2 = np.float32(np.log2(np.e) / np.sqrt(HEAD_DIM))


def _rms_norm_rows(x, g):
    ms = jnp.mean(x * x, axis=-1, keepdims=True)
    return x * lax.rsqrt(ms + EPS) * g


def _layer_norm_rows(x, g, b):
    mu = jnp.mean(x, axis=-1, keepdims=True)
    xc = x - mu
    var = jnp.mean(xc * xc, axis=-1, keepdims=True)
    return xc * lax.rsqrt(var + EPS) * g + b


def _rope_tables(seq):
    pos = jnp.arange(seq, dtype=F32)
    inv = ROPE_THETA ** (-jnp.arange(0, ROT_DIM, 2, dtype=F32) / ROT_DIM)
    ang = pos[:, None] * inv[None, :]
    cos, sin = jnp.cos(ang), jnp.sin(ang)
    rest = HEAD_DIM - ROT_DIM
    cos_t = jnp.concatenate([cos, cos, jnp.ones((seq, rest), F32)], axis=-1)
    sin_a = jnp.concatenate([-sin, jnp.zeros((seq, HEAD_DIM - ROT_HALF), F32)], axis=-1)
    sin_b = jnp.concatenate([jnp.zeros((seq, ROT_HALF), F32), sin, jnp.zeros((seq, rest), F32)], axis=-1)
    return cos_t, sin_a, sin_b


class _SlabCaster:
    def __init__(self, layer, src_refs, dst_refs, in_bufs, out_bufs, sems):
        self.layer, self.src_refs, self.dst_refs = layer, src_refs, dst_refs
        self.in_bufs, self.out_bufs, self.sems = in_bufs, out_bufs, sems

    def __len__(self):
        return len(self.src_refs)

    def _read(self, a, step):
        rows = self.in_bufs[a].shape[0]
        return pltpu.make_async_copy(self.src_refs[a].at[self.layer, pl.ds(step * rows, rows), :],
                                     self.in_bufs[a], self.sems.at[0, a])

    def _write(self, a, step):
        rows = self.out_bufs[a].shape[0]
        return pltpu.make_async_copy(self.out_bufs[a], self.dst_refs[a].at[pl.ds(step * rows, rows), :],
                                     self.sems.at[1, a])

    def prime(self):
        for a in range(len(self)):
            self._read(a, 0).start()
            self.out_bufs[a][...] = jnp.zeros(self.out_bufs[a].shape, BF16)
            self._write(a, 0).start()

    def cast(self, a, step, n_steps):
        self._read(a, step).wait()
        self._write(a, jnp.maximum(step - 1, 0)).wait()
        self.out_bufs[a][...] = self.in_bufs[a][...].astype(BF16)
        self._write(a, step).start()
        self._read(a, jnp.minimum(step + 1, n_steps - 1)).start()

    def drain(self, step):
        for a in range(len(self)):
            self._read(a, step).wait()
            self._write(a, step).wait()


def _slab_cast_specs(mats_f32, shapes, n_steps):
    slabs = []
    for w, (rows, cols) in zip(mats_f32, shapes):
        assert w.shape[1:] == (rows, cols) and rows % (n_steps * BF16_SUBLANE_TILE) == 0
        slabs.append((rows // n_steps, cols))
    out_shapes = [jax.ShapeDtypeStruct(shape, BF16) for shape in shapes[:len(mats_f32)]]
    scratch = [pltpu.VMEM(slab, F32) for slab in slabs] + [pltpu.VMEM(slab, BF16) for slab in slabs]
    if slabs:
        scratch.append(pltpu.SemaphoreType.DMA((2, len(slabs))))
    return out_shapes, scratch


def _in_proj_kernel(cast_layer, n_cast, *refs):
    x_ref, g_ref, cos_ref, sa_ref, sb_ref, w_ref = refs[:6]
    src_refs = refs[6:6 + n_cast]
    q_ref, kv_ref, c_ref, gu_ref = refs[6 + n_cast:10 + n_cast]
    dst_refs = refs[10 + n_cast:10 + 2 * n_cast]
    h_scr = refs[10 + 2 * n_cast]
    in_bufs = refs[11 + 2 * n_cast:11 + 3 * n_cast]
    out_bufs = refs[11 + 3 * n_cast:11 + 4 * n_cast]
    caster = _SlabCaster(cast_layer, src_refs, dst_refs, in_bufs, out_bufs,
                         refs[11 + 4 * n_cast] if n_cast else None)
    step, n_steps = pl.program_id(0), pl.num_programs(0)
    if n_cast:
        pl.when(step == 0)(caster.prime)
    tm = x_ref.shape[0]

    def rope(t, rows):
        return (t * cos_ref[rows, :] + pltpu.roll(t, HEAD_DIM - ROT_HALF, 1) * sa_ref[rows, :]
                + pltpu.roll(t, ROT_HALF, 1) * sb_ref[rows, :])

    def proj(lo, hi, rows=slice(None)):
        return jnp.dot(h_scr[rows, :], w_ref[:, lo:hi], preferred_element_type=F32)

    for half in range(2):
        rows = slice(half * tm // 2, (half + 1) * tm // 2)
        h_scr[rows, :] = _rms_norm_rows(x_ref[rows, :], g_ref[...]).astype(BF16)
        zq = proj(0, K_START, rows)
        for hd in range(N_Q_HEADS):
            sl = slice(hd * HEAD_DIM, (hd + 1) * HEAD_DIM)
            q_ref[rows, sl] = (rope(zq[:, sl], rows) * SCORE_SCALE_LOG2).astype(BF16)
    zk = proj(K_START, V_START)
    for hd in range(N_KV_HEADS):
        sl = slice(hd * HEAD_DIM, (hd + 1) * HEAD_DIM)
        kv_ref[:, sl] = rope(zk[:, sl], slice(None)).astype(BF16)
    kv_ref[:, KV_WIDTH:] = proj(V_START, CA_START).astype(BF16)
    za = proj(CA_START, CG_START)
    zg = proj(CG_START, UV_START)
    c_ref[...] = (za * jax.nn.sigmoid(zg)).astype(BF16)
    zu = proj(UV_START, IN_WIDTH)
    gelu = 0.5 * zu * (1.0 + lax.erf(zu * np.float32(np.sqrt(0.5))))
    gu_ref[...] = gelu.astype(BF16)

    for a in range(n_cast):
        caster.cast(a, step, n_steps)
    if n_cast:
        pl.when(step == n_steps - 1)(lambda: caster.drain(step))


def _in_proj(layer, x, norm_g, ropes, w_in, seq, cast_f32=()):
    tokens = x.shape[0]
    tm = TM_PROJ
    n_steps = tokens // tm
    seq_blocks = seq // tm
    row = lambda i: (i, 0)
    hbm = pl.BlockSpec(memory_space=pl.ANY)
    rope_spec = pl.BlockSpec((tm, HEAD_DIM), lambda i: (i % seq_blocks, 0))
    out_w = (ATTN_WIDTH, 2 * KV_WIDTH, CONV_WIDTH, 2 * SGU_WIDTH)
    cast_shapes, cast_scratch = _slab_cast_specs(cast_f32, NEXT_WEIGHT_SHAPES[1:], n_steps)
    outs = pl.pallas_call(
        functools.partial(_in_proj_kernel, layer, len(cast_f32)),
        grid=(n_steps,),
        in_specs=[
            pl.BlockSpec((tm, D_MODEL), row),
            pl.BlockSpec((None, 1, D_MODEL), lambda i: (layer, 0, 0)),
            rope_spec, rope_spec, rope_spec,
            pl.BlockSpec((D_MODEL, IN_WIDTH), lambda i: (0, 0), pipeline_mode=pl.Buffered(1)),
        ] + [hbm] * len(cast_f32),
        out_specs=[pl.BlockSpec((tm, w), row) for w in out_w] + [hbm] * len(cast_f32),
        out_shape=[jax.ShapeDtypeStruct((tokens, w), BF16) for w in out_w] + cast_shapes,
        scratch_shapes=[pltpu.VMEM((tm, D_MODEL), BF16)] + cast_scratch,
        compiler_params=pltpu.CompilerParams(
            dimension_semantics=("arbitrary",), vmem_limit_bytes=V7X_VMEM_LIMIT_BYTES),
        name=f"in_proj_l{layer}",
    )(x, norm_g, *ropes, w_in, *cast_f32)
    return tuple(outs[:len(out_w)]), tuple(outs[len(out_w):])


def _mixer_kernel(layer, sink_ref, x_ref, q_ref, kvc_ref, kvp_ref, kvn_ref, cc_ref, cp_ref, cn_ref,
                  gu_ref, dww_ref, dwb_ref, clg_ref, clb_ref, slg_ref, slb_ref, sw_ref, sbt_ref, wo_ref,
                  o_ref, mix_scr, kv_scr, c_scr, wb_scr, y_scr):
    tb = q_ref.shape[0]
    conv_col = ATTN_WIDTH
    sgu_col = ATTN_WIDTH + CONV_WIDTH

    i = pl.program_id(1)
    is_first = (i == 0).astype(jnp.int32)
    is_last = (i == pl.num_programs(1) - 1).astype(jnp.int32)

    for src, lo, hi in ((kvp_ref, 0, BLOCK), (kvc_ref, BLOCK, BLOCK + tb),
                        (kvn_ref, BLOCK + tb, 2 * BLOCK + tb)):
        kv_scr[lo:hi, 0:KV_WIDTH] = src[:, 0:KV_WIDTH]
        for g in range(N_KV_HEADS):
            dst = KV_WIDTH + g * 2 * HEAD_DIM
            kv_scr[lo:hi, dst:dst + HEAD_DIM] = src[:, KV_WIDTH + g * HEAD_DIM:KV_WIDTH + (g + 1) * HEAD_DIM]
            kv_scr[lo:hi, dst + HEAD_DIM:dst + 2 * HEAD_DIM] = jnp.ones((hi - lo, HEAD_DIM), BF16)
    r = lax.broadcasted_iota(jnp.int32, (BLOCK, BLOCK), 0)
    c = lax.broadcasted_iota(jnp.int32, (BLOCK, BLOCK), 1)
    prev_bias = jnp.where(c >= r, 0.0, MASK_VALUE).astype(F32)
    next_bias = jnp.where(c <= r, 0.0, MASK_VALUE).astype(F32)
    prev_bias_first = jnp.minimum(prev_bias, MASK_VALUE * is_first.astype(F32))
    next_bias_last = jnp.minimum(next_bias, MASK_VALUE * is_last.astype(F32))
    n_sub = tb // BLOCK
    for j in range(n_sub):
        lo_bias = prev_bias_first if j == 0 else prev_bias
        hi_bias = next_bias_last if j == n_sub - 1 else next_bias
        rows = slice(j * BLOCK, (j + 1) * BLOCK)
        win = slice(j * BLOCK, (j + 3) * BLOCK)
        for g in range(N_KV_HEADS):
            heads = range(g * Q_PER_KV, (g + 1) * Q_PER_KV)
            qg = jnp.concatenate(
                [q_ref[rows, hd * HEAD_DIM:(hd + 1) * HEAD_DIM] for hd in heads], axis=0)
            kwin = kv_scr[win, g * HEAD_DIM:(g + 1) * HEAD_DIM]
            v_col = KV_WIDTH + g * 2 * HEAD_DIM
            v_ones = kv_scr[win, v_col:v_col + 2 * HEAD_DIM]
            s = lax.dot_general(qg, kwin, (((1,), (1,)), ((), ())), preferred_element_type=F32)
            probs, sink_terms = [], []
            for hh, hd in enumerate(heads):
                sink = sink_ref[layer, hd] * LOG2_E
                hrows = slice(hh * BLOCK, (hh + 1) * BLOCK)
                sh = jnp.concatenate([s[hrows, 0:BLOCK] + lo_bias, s[hrows, BLOCK:2 * BLOCK],
                                      s[hrows, 2 * BLOCK:] + hi_bias], axis=1)
                m = jnp.maximum(jnp.max(sh, axis=-1, keepdims=True), sink)
                probs.append(jnp.exp2(sh - m).astype(BF16))
                sink_terms.append(jnp.exp2(sink - m))
            o = jnp.dot(jnp.concatenate(probs, axis=0), v_ones, preferred_element_type=F32)
            for hh, hd in enumerate(heads):
                hrows = slice(hh * BLOCK, (hh + 1) * BLOCK)
                denom = o[hrows, HEAD_DIM:] + sink_terms[hh]
                oh = o[hrows, 0:HEAD_DIM] * (1.0 / denom)
                mix_scr[rows, hd * HEAD_DIM:(hd + 1) * HEAD_DIM] = oh.astype(BF16)

    for ci in range(tb // CHUNK):
        rows = slice(ci * CHUNK, (ci + 1) * CHUNK)
        v = _layer_norm_rows(gu_ref[rows, SGU_WIDTH:].astype(F32), slg_ref[...], slb_ref[...])
        v = v.astype(BF16)
        for hd in range(SGU_HEADS):
            cols = slice(hd * HEAD_DIM, (hd + 1) * HEAD_DIM)
            sp = jnp.dot(sw_ref[hd], v[:, cols], preferred_element_type=F32) + sbt_ref[:, hd:hd + 1]
            u = gu_ref[rows, cols].astype(F32)
            mix_scr[rows, sgu_col + hd * HEAD_DIM:sgu_col + (hd + 1) * HEAD_DIM] = (u * sp).astype(BF16)

    halo = cp_ref.shape[0]
    c_scr[0, 0:halo, :] = cp_ref[...].astype(F32) * (1 - is_first).astype(F32)
    c_scr[0, halo:halo + tb, :] = cc_ref[...].astype(F32)
    c_scr[0, halo + tb:, :] = cn_ref[...].astype(F32) * (1 - is_last).astype(F32)
    shifted_rows = tb + 2 * halo - F32_SUBLANE_TILE
    for s in range(1, F32_SUBLANE_TILE):
        c_scr[s, 0:shifted_rows, :] = c_scr[0, s:s + shifted_rows, :]
    first_tap = halo - CONV_PAD

    for k in range(CONV_KERNEL):
        wb_scr[k] = jnp.broadcast_to(dww_ref[k:k + 1, :], (F32_SUBLANE_TILE, CONV_WIDTH))
    wb_scr[CONV_KERNEL] = jnp.broadcast_to(dwb_ref[...], (F32_SUBLANE_TILE, CONV_WIDTH))
    groups = CONV_ROWS // F32_SUBLANE_TILE

    def conv_rows(t, carry):
        r0 = pl.multiple_of(t * CONV_ROWS, CONV_ROWS)
        acc = [wb_scr[CONV_KERNEL]] * groups
        for k in range(CONV_KERNEL):
            shift = (first_tap + k) % F32_SUBLANE_TILE
            w = wb_scr[k]
            for gi in range(groups):
                start = pl.multiple_of(
                    r0 + (first_tap + k - shift + gi * F32_SUBLANE_TILE), F32_SUBLANE_TILE)
                acc[gi] = acc[gi] + c_scr[shift, pl.ds(start, F32_SUBLANE_TILE), :] * w
        for gi in range(groups):
            out_row = pl.multiple_of(r0 + gi * F32_SUBLANE_TILE, F32_SUBLANE_TILE)
            y_scr[pl.ds(out_row, F32_SUBLANE_TILE), :] = acc[gi]
        return carry

    lax.fori_loop(0, tb // CONV_ROWS, conv_rows, 0, unroll=4)
    y = _layer_norm_rows(y_scr[...], clg_ref[...], clb_ref[...])
    mix_scr[:, conv_col:sgu_col] = (y * jax.nn.sigmoid(y)).astype(BF16)

    o_ref[...] = x_ref[...] + jnp.dot(mix_scr[...], wo_ref[...], preferred_element_type=F32)


def _mixer(layer, sink, x, q, kv, c, gu, dww, dwb, clg, clb, slg, slb, sgu_w, sgu_bt, w_out):
    batch, seq, _ = q.shape
    tb = TB_MIX
    halo = BF16_SUBLANE_TILE
    assert CONV_PAD <= halo and tb % BLOCK == 0 and tb % CONV_ROWS == 0
    kv_per = tb // BLOCK
    c_per = tb // halo
    cur = lambda b, i: (b, i, 0)
    kv_prev = lambda b, i: (b, jnp.maximum(i * kv_per - 1, 0), 0)
    kv_next = lambda b, i: (b, jnp.minimum((i + 1) * kv_per, seq // BLOCK - 1), 0)
    c_prev = lambda b, i: (b, jnp.maximum(i * c_per - 1, 0), 0)
    c_next = lambda b, i: (b, jnp.minimum((i + 1) * c_per, seq // halo - 1), 0)
    per_layer = lambda b, i: (layer, 0, 0)
    vec = lambda w: pl.BlockSpec((None, 1, w), per_layer)
    return pl.pallas_call(
        functools.partial(_mixer_kernel, layer),
        grid=(batch, seq // tb),
        in_specs=[
            pl.BlockSpec(memory_space=pltpu.SMEM),
            pl.BlockSpec((None, tb, D_MODEL), cur),
            pl.BlockSpec((None, tb, ATTN_WIDTH), cur),
            pl.BlockSpec((None, tb, 2 * KV_WIDTH), cur),
            pl.BlockSpec((None, BLOCK, 2 * KV_WIDTH), kv_prev),
            pl.BlockSpec((None, BLOCK, 2 * KV_WIDTH), kv_next),
            pl.BlockSpec((None, tb, CONV_WIDTH), cur),
            pl.BlockSpec((None, halo, CONV_WIDTH), c_prev),
            pl.BlockSpec((None, halo, CONV_WIDTH), c_next),
            pl.BlockSpec((None, tb, 2 * SGU_WIDTH), cur),
            pl.BlockSpec((None, CONV_KERNEL, CONV_WIDTH), per_layer),
            vec(CONV_WIDTH), vec(CONV_WIDTH), vec(CONV_WIDTH), vec(SGU_WIDTH), vec(SGU_WIDTH),
            pl.BlockSpec((None, SGU_HEADS, CHUNK, CHUNK), lambda b, i: (layer, 0, 0, 0)),
            pl.BlockSpec((None, CHUNK, SGU_HEADS), per_layer),
            pl.BlockSpec((D_MODEL, D_MODEL), lambda b, i: (0, 0), pipeline_mode=pl.Buffered(1)),
        ],
        out_specs=pl.BlockSpec((None, tb, D_MODEL), cur),
        out_shape=jax.ShapeDtypeStruct((batch, seq, D_MODEL), F32),
        scratch_shapes=[
            pltpu.VMEM((tb, D_MODEL), BF16),
            pltpu.VMEM((tb + 2 * BLOCK, 3 * KV_WIDTH), BF16),
            pltpu.VMEM((F32_SUBLANE_TILE, tb + 2 * halo, CONV_WIDTH), F32),
            pltpu.VMEM((CONV_KERNEL + 1, F32_SUBLANE_TILE, CONV_WIDTH), F32),
            pltpu.VMEM((tb, CONV_WIDTH), F32),
        ],
        compiler_params=pltpu.CompilerParams(
            dimension_semantics=("arbitrary", "arbitrary"), vmem_limit_bytes=V7X_VMEM_LIMIT_BYTES),
        name=f"mixer_l{layer}",
    )(sink, x, q, kv, kv, kv, c, c, c, gu, dww, dwb, clg, clb, slg, slb, sgu_w, sgu_bt, w_out)


def _ffn_kernel(next_layer, apply_final_norm, *refs):
    n_mats = len(NEXT_WEIGHT_SHAPES) if next_layer is not None else 0
    x_ref, g_ref, fg_ref, wg_hbm, wu_hbm, wd_hbm = refs[:6]
    o_ref = refs[6 + n_mats]
    h_scr, wgu_buf, wd_buf, sems = refs[7 + 2 * n_mats:11 + 2 * n_mats]
    caster = _SlabCaster(
        next_layer, refs[6:6 + n_mats], refs[7 + n_mats:7 + 2 * n_mats],
        refs[11 + 2 * n_mats:11 + 3 * n_mats], refs[11 + 3 * n_mats:11 + 4 * n_mats],
        refs[11 + 4 * n_mats] if n_mats else None)

    i = pl.program_id(0)
    n_tiles = pl.num_programs(0)
    chunks = _ffn_chunks()
    n_chunks = len(chunks)

    def gate_up_copies(c):
        slot, (lo, width) = c % FFN_SLOTS, chunks[c]
        return (
            pltpu.make_async_copy(wg_hbm.at[:, pl.ds(lo, width)],
                                  wgu_buf.at[slot, :, pl.ds(0, width)], sems.at[0, slot]),
            pltpu.make_async_copy(wu_hbm.at[:, pl.ds(lo, width)],
                                  wgu_buf.at[slot, :, pl.ds(width, width)], sems.at[1, slot]),
        )

    def down_copies(c):
        slot, (lo, width) = c % FFN_SLOTS, chunks[c]
        return (
            pltpu.make_async_copy(wd_hbm.at[pl.ds(lo, width), :],
                                  wd_buf.at[slot, pl.ds(0, width), :], sems.at[2, slot]),
        )

    def start(copies_of, c):
        for copy in copies_of(c % n_chunks):
            copy.start()

    def wait(copies_of, c):
        for copy in copies_of(c):
            copy.wait()

    @pl.when(i == 0)
    def _():
        start(gate_up_copies, 0)
        start(gate_up_copies, 1)
        start(down_copies, 0)
        caster.prime()

    tm = x_ref.shape[0]
    wait(gate_up_copies, 0)

    def gate_up_act(rows, slot, width):
        gate_up = jnp.dot(h_scr[rows, :], wgu_buf[slot, :, 0:2 * width], preferred_element_type=F32)
        gate, up = gate_up[:, 0:width], gate_up[:, width:]
        return (gate * jax.nn.sigmoid(gate) * up).astype(BF16)

    first_groups = [slice(g * tm // FFN_ROW_SPLIT, (g + 1) * tm // FFN_ROW_SPLIT)
                    for g in range(FFN_ROW_SPLIT)]
    for c, (_, width) in enumerate(chunks):
        slot = c % FFN_SLOTS
        halves = first_groups if c == 0 else [slice(0, tm)]
        acts = []
        for rows in halves:
            if c == 0:
                x = x_ref[rows, :]
                o_ref[rows, :] = x
                h_scr[rows, :] = _rms_norm_rows(x, g_ref[...]).astype(BF16)
            acts.append(gate_up_act(rows, slot, width))
        start(gate_up_copies, c + 2)
        start(down_copies, c + 1)
        if c + 1 < n_chunks:
            wait(gate_up_copies, c + 1)
        wait(down_copies, c)
        if c < n_mats:
            caster.cast(c, i, n_tiles)
        for rows, act in zip(halves, acts):
            o_ref[rows, :] += jnp.dot(act, wd_buf[slot, 0:width, :], preferred_element_type=F32)

    if apply_final_norm:
        o_ref[...] = _rms_norm_rows(o_ref[...], fg_ref[...])

    @pl.when(i == n_tiles - 1)
    def _():
        wait(gate_up_copies, 0)
        wait(gate_up_copies, 1)
        wait(down_copies, 0)
        caster.drain(i)


def _ffn_chunks():
    bounds = list(range(0, D_FF, TF_FFN)) + [D_FF]
    return tuple((lo, hi - lo) for lo, hi in zip(bounds[:-1], bounds[1:]))


def _ffn(layer, x, norm_g, w_gate, w_up, w_down, final_g, apply_final_norm, next_f32=()):
    tokens = x.shape[0]
    tm, tf = TM_FFN, TF_FFN
    n_tiles = tokens // tm
    assert FFN_SLOTS == 2 and len(_ffn_chunks()) % FFN_SLOTS == 0
    assert len(next_f32) in (0, len(NEXT_WEIGHT_SHAPES)) and len(next_f32) <= len(_ffn_chunks())
    cast_shapes, cast_scratch = _slab_cast_specs(next_f32, NEXT_WEIGHT_SHAPES, n_tiles)
    row = lambda i: (i, 0)
    hbm = pl.BlockSpec(memory_space=pl.ANY)
    outs = pl.pallas_call(
        functools.partial(_ffn_kernel, layer + 1 if next_f32 else None, apply_final_norm),
        grid=(n_tiles,),
        in_specs=[
            pl.BlockSpec((tm, D_MODEL), row),
            pl.BlockSpec((None, 1, D_MODEL), lambda i: (layer, 0, 0)),
            pl.BlockSpec((1, D_MODEL), lambda i: (0, 0)),
            hbm, hbm, hbm,
        ] + [hbm] * len(next_f32),
        out_specs=[pl.BlockSpec((tm, D_MODEL), row)] + [hbm] * len(next_f32),
        out_shape=[jax.ShapeDtypeStruct((tokens, D_MODEL), F32)] + cast_shapes,
        scratch_shapes=[
            pltpu.VMEM((tm, D_MODEL), BF16),
            pltpu.VMEM((FFN_SLOTS, D_MODEL, 2 * tf), BF16),
            pltpu.VMEM((FFN_SLOTS, tf, D_MODEL), BF16),
            pltpu.SemaphoreType.DMA((3, FFN_SLOTS)),
        ] + cast_scratch,
        compiler_params=pltpu.CompilerParams(
            dimension_semantics=("arbitrary",), vmem_limit_bytes=V7X_VMEM_LIMIT_BYTES),
        name=f"ffn_l{layer}",
    )(x, norm_g, final_g, w_gate, w_up, w_down, *next_f32)
    return outs[0], tuple(outs[1:])


@jax.jit
def _forward(x, mix_norm_g, w_in, sink, conv_dw_w, conv_dw_b, conv_ln_g, conv_ln_b,
             sgu_ln_g, sgu_ln_b, sgu_w, sgu_b, w_out, ffn_norm_g, w_gate, w_up, w_down,
             final_norm_g):
    batch, seq, d = x.shape
    tokens = batch * seq
    ropes = _rope_tables(seq)
    weights_f32 = (w_in, w_out, w_gate, w_up, w_down)
    weights = (w_in[0].astype(BF16),)
    sgu_w = sgu_w.astype(BF16)
    vec3 = lambda p: p.reshape(DEPTH, 1, p.shape[-1])
    mix_norm_g, ffn_norm_g = vec3(mix_norm_g), vec3(ffn_norm_g)
    conv_dw_b, conv_ln_g, conv_ln_b = vec3(conv_dw_b), vec3(conv_ln_g), vec3(conv_ln_b)
    sgu_ln_g, sgu_ln_b = vec3(sgu_ln_g), vec3(sgu_ln_b)
    sgu_bt = jnp.swapaxes(sgu_b, 1, 2)
    final_g = final_norm_g.reshape(1, d)

    xf = x.reshape(tokens, d)
    for layer in range(DEPTH):
        last = layer == DEPTH - 1
        (q, kv, c, gu), cast_now = _in_proj(layer, xf, mix_norm_g, ropes, weights[0], seq,
                                            cast_f32=weights_f32[1:] if layer == 0 else ())
        _, w_out_l, w_gate_l, w_up_l, w_down_l = weights + cast_now
        to3 = lambda a: a.reshape(batch, seq, a.shape[-1])
        xn = _mixer(layer, sink, to3(xf), to3(q), to3(kv), to3(c), to3(gu), conv_dw_w, conv_dw_b,
                    conv_ln_g, conv_ln_b, sgu_ln_g, sgu_ln_b, sgu_w, sgu_bt, w_out_l)
        xf, weights = _ffn(layer, xn.reshape(tokens, d), ffn_norm_g, w_gate_l, w_up_l, w_down_l,
                           final_g, apply_final_norm=last, next_f32=() if last else weights_f32)
    return xf.reshape(batch, seq, d)


def kernel(x, mix_norm_g, w_in, sink, conv_dw_w, conv_dw_b, conv_ln_g, conv_ln_b, sgu_ln_g, sgu_ln_b,
           sgu_w, sgu_b, w_out, ffn_norm_g, w_gate, w_up, w_down, final_norm_g):
    return _forward(x, mix_norm_g, w_in, sink, conv_dw_w, conv_dw_b, conv_ln_g, conv_ln_b,
                    sgu_ln_g, sgu_ln_b, sgu_w, sgu_b, w_out, ffn_norm_g, w_gate, w_up, w_down,
                    final_norm_g)
```
